```python
import jax
import jax.numpy as jnp
from jax import lax
import numpy as np

D_MODEL = 2048
BATCH = 4
SEQ = 2048
DEPTH = 4

N_A_LAYERS = DEPTH // 2
N_B_LAYERS = DEPTH - N_A_LAYERS
HG_EXPAND = 128
HG_HEADS = D_MODEL // HG_EXPAND
HG_DK = HG_EXPAND
HG_DV = D_MODEL // HG_HEADS
HG_CHUNK = 64
LB_FLOOR = 1e-30
ATT_HEAD_DIM = 64
ATT_Q_HEADS = D_MODEL // ATT_HEAD_DIM
ATT_KV_HEADS = ATT_Q_HEADS // 8
ATT_GROUP = ATT_Q_HEADS // ATT_KV_HEADS
WINDOW = 128
ROT_DIM = ATT_HEAD_DIM // 4
ROPE_THETA = 500000.0
MASK_VALUE = -1e30
N_EXPERTS = 32
N_GROUPS = 8
EXPERTS_PER_GROUP = N_EXPERTS // N_GROUPS
TOP_K = 2
GROUP_SCORE_K = 2
D_FF = (704 * D_MODEL) // 2048
MOE_BLOCK = 128
ROUTE_MASK = -1e9
DEEPNORM_ALPHA = (2 * DEPTH) ** 0.25
DEEPNORM_BETA = (8 * DEPTH) ** -0.25
LN_EPS = 1e-5
RMS_EPS = 1e-6

kernel_name = 'hybrid_yoco_hgrn2_swa_sink_grouped_moe'


def layer_norm(x, g, b):
    xf = x.astype(jnp.float32)
    mu = jnp.mean(xf, axis=-1, keepdims=True)
    var = jnp.mean(jnp.square(xf - mu), axis=-1, keepdims=True)
    return ((xf - mu) * lax.rsqrt(var + LN_EPS)).astype(x.dtype) * g + b


def partial_rotary(t, positions):
    half = ROT_DIM // 2
    inv_freq = 1.0 / (ROPE_THETA ** (jnp.arange(half, dtype=jnp.float32) * 2.0 / ROT_DIM))
    ang = positions.astype(jnp.float32)[..., None] * inv_freq
    cos = jnp.cos(ang)[:, :, None, :]
    sin = jnp.sin(ang)[:, :, None, :]
    t1 = t[..., :half].astype(jnp.float32)
    t2 = t[..., half:ROT_DIM].astype(jnp.float32)
    rot = jnp.concatenate([t1 * cos - t2 * sin, t2 * cos + t1 * sin], axis=-1).astype(t.dtype)
    return jnp.concatenate([rot, t[..., ROT_DIM:]], axis=-1)


def hgrn2_chunk_scan(q, k, v, log_f):
    B, H, S, DK = q.shape
    DV = v.shape[-1]
    C = HG_CHUNK
    N = S // C

    def to_chunks(t):
        return t.astype(jnp.float32).reshape(B, H, N, C, t.shape[-1]).transpose(2, 0, 1, 3, 4)

    qc, kc, vc, gc = to_chunks(q), to_chunks(k), to_chunks(v), to_chunks(log_f)
    causal = jnp.tril(jnp.ones((C, C), dtype=bool))[:, :, None]

    def step(state, inp):
        qi, ki, vi, gi = inp
        b = jnp.cumsum(gi, axis=2)
        b_last = b[:, :, -1:, :]
        o_inter = jnp.einsum('bhck,bhkv->bhcv', qi * jnp.exp(b), state)
        diff = b[:, :, :, None, :] - b[:, :, None, :, :]
        decay = jnp.where(causal, jnp.exp(jnp.where(causal, diff, 0.0)), 0.0)
        scores = jnp.einsum('bhik,bhjk,bhijk->bhij', qi, ki, decay)
        o = o_inter + jnp.einsum('bhij,bhjv->bhiv', scores, vi)
        new_state = jnp.exp(b_last[:, :, 0, :])[..., None] * state + jnp.einsum(
            'bhjk,bhjv->bhkv', ki * jnp.exp(b_last - b), vi)
        return new_state, o

    state0 = jnp.zeros((B, H, DK, DV), jnp.float32)
    _, o = lax.scan(step, state0, (qc, kc, vc, gc))
    return o.transpose(1, 2, 0, 3, 4).reshape(B, H, S, DV)


def hgrn2_mixer(x, w_in, lower_bound, g_norm_w, w_out):
    B, S, D = x.shape
    q, f, i, g = jnp.split(x @ w_in, 4, axis=-1)
    q = jax.nn.silu(q) * (HG_DK ** -0.5)
    fg = f.astype(jnp.float32)
    lb = lower_bound.astype(jnp.float32)
    log_f = jnp.logaddexp(jnp.log(jnp.maximum(lb, LB_FLOOR)), jnp.log1p(-lb) + jax.nn.log_sigmoid(fg))
    k = (1.0 - lb) * jax.nn.sigmoid(-fg)

    def heads(t):
        return t.reshape(B, S, HG_HEADS, -1).transpose(0, 2, 1, 3)

    o = hgrn2_chunk_scan(heads(q), heads(k), heads(i), heads(log_f))
    o = o.transpose(0, 2, 1, 3)
    o = (o * lax.rsqrt(jnp.mean(jnp.square(o), axis=-1, keepdims=True) + RMS_EPS)).astype(x.dtype)
    o = (o * g_norm_w).reshape(B, S, D) * jax.nn.silu(g)
    return o @ w_out


def swa_sink_attention(q, k, v, sinks):
    B, S, HQ, hd = q.shape
    NB = S // WINDOW
    qb = q.reshape(B, NB, WINDOW, ATT_KV_HEADS, ATT_GROUP, hd)

    def band(t):
        tb = t.reshape(B, NB, WINDOW, ATT_KV_HEADS, hd)
        prev = jnp.pad(tb, ((0, 0), (1, 0), (0, 0), (0, 0), (0, 0)))[:, :-1]
        return jnp.concatenate([prev, tb], axis=2)

    kb, vb = band(k), band(v)
    scores = jnp.einsum('bnqhgd,bnkhd->bnhgqk', qb, kb).astype(jnp.float32) * (hd ** -0.5)
    q_idx = jnp.arange(WINDOW)[:, None] + WINDOW
    k_idx = jnp.arange(2 * WINDOW)[None, :]
    rel = q_idx - k_idx
    in_window = (rel >= 0) & (rel < WINDOW)
    has_prev = (jnp.arange(NB) > 0)[:, None, None] | (k_idx >= WINDOW)[None]
    mask = in_window[None] & has_prev
    scores = jnp.where(mask[None, :, None, None], scores, MASK_VALUE)
    sink = sinks.astype(jnp.float32).reshape(ATT_KV_HEADS, ATT_GROUP)[None, None, :, :, None, None]
    m = jnp.maximum(jnp.max(scores, axis=-1, keepdims=True), sink)
    p = jnp.exp(scores - m)
    p = p / (jnp.sum(p, axis=-1, keepdims=True) + jnp.exp(sink - m))
    out = jnp.einsum('bnhgqk,bnkhd->bnqhgd', p.astype(v.dtype), vb)
    return out.reshape(B, S, HQ, hd)


def shared_kv(x, w_kv, b_kv, positions):
    B, S, _ = x.shape
    k, v = jnp.split(x @ w_kv + b_kv, 2, axis=-1)
    k = partial_rotary(k.reshape(B, S, ATT_KV_HEADS, ATT_HEAD_DIM), positions)
    v = v.reshape(B, S, ATT_KV_HEADS, ATT_HEAD_DIM)
    return k, v


def swa_mixer(x, k, v, w_q, b_q, sinks, w_o, b_o, positions):
    B, S, D = x.shape
    q = (x @ w_q + b_q).reshape(B, S, ATT_Q_HEADS, ATT_HEAD_DIM)
    q = partial_rotary(q, positions)
    o = swa_sink_attention(q, k, v, sinks)
    return o.reshape(B, S, ATT_Q_HEADS * ATT_HEAD_DIM) @ w_o + b_o


def grouped_route(x2, w_router, router_bias):
    N = x2.shape[0]
    scores = jax.nn.sigmoid((x2 @ w_router).astype(jnp.float32))
    biased = scores + router_bias.astype(jnp.float32)
    grp = biased.reshape(N, N_GROUPS, EXPERTS_PER_GROUP)
    group_score = jnp.sum(lax.top_k(grp, GROUP_SCORE_K)[0], axis=-1)
    best = jnp.argmax(group_score, axis=-1)
    in_group = (jnp.arange(N_EXPERTS) // EXPERTS_PER_GROUP)[None, :] == best[:, None]
    _, idx = lax.top_k(jnp.where(in_group, biased, ROUTE_MASK), TOP_K)
    w = jnp.take_along_axis(scores, idx, axis=-1)
    return idx, w / jnp.sum(w, axis=-1, keepdims=True)


def moe_ffn(x, w_router, router_bias, w_gate_up, w_down):
    B, S, D = x.shape
    N = B * S
    x2 = x.reshape(N, D)
    idx, gate = grouped_route(x2, w_router, router_bias)
    NK = N * TOP_K
    e_flat = idx.reshape(NK)
    tok_flat = jnp.arange(NK) // TOP_K
    g_flat = gate.reshape(NK)
    order = jnp.argsort(e_flat)
    e_sorted, tok_sorted, g_sorted = e_flat[order], tok_flat[order], g_flat[order]
    counts = jnp.zeros((N_EXPERTS,), jnp.int32).at[e_flat].add(1)
    padded = (counts + MOE_BLOCK - 1) // MOE_BLOCK * MOE_BLOCK
    pad_end = jnp.cumsum(padded)
    pad_start = pad_end - padded
    start = jnp.cumsum(counts) - counts
    dest = pad_start[e_sorted] + jnp.arange(NK) - start[e_sorted]
    n_blocks = (NK + N_EXPERTS * (MOE_BLOCK - 1) + MOE_BLOCK - 1) // MOE_BLOCK
    P = n_blocks * MOE_BLOCK
    x_pad = jnp.zeros((P, D), x.dtype).at[dest].set(x2[tok_sorted])
    block_expert = jnp.minimum(
        jnp.searchsorted(pad_end, jnp.arange(n_blocks) * MOE_BLOCK, side='right'), N_EXPERTS - 1)

    def expert_block(args):
        xb, e = args
        hg, hu = jnp.split(xb @ w_gate_up[e], 2, axis=-1)
        return (jax.nn.silu(hg) * hu) @ w_down[e]

    y_pad = lax.map(expert_block, (x_pad.reshape(n_blocks, MOE_BLOCK, D), block_expert))
    y = y_pad.reshape(P, D)[dest] * g_sorted[:, None].astype(x.dtype)
    out = jnp.zeros((N, D), x.dtype).at[tok_sorted].add(y)
    return out.reshape(B, S, D)


def setup_inputs(seed: int = 0) -> dict:
    key = jax.random.key(seed)
    ks = jax.random.split(key, 24)
    D = D_MODEL
    f32 = jnp.float32
    nrm = jax.random.normal
    q_width = ATT_Q_HEADS * ATT_HEAD_DIM
    kv_width = 2 * ATT_KV_HEADS * ATT_HEAD_DIM
    x = nrm(ks[0], (BATCH, SEQ, D), f32)
    offset = jax.random.randint(ks[1], (BATCH, 1), 0, 4096, dtype=jnp.int32)
    positions = offset + jnp.arange(SEQ, dtype=jnp.int32)[None, :]
    return {
        'x': x,
        'positions': positions,
        'w_in_hgrn': nrm(ks[2], (N_A_LAYERS, D, 4 * D), f32) * D ** -0.5,
        'lb_param': 0.5 * nrm(ks[3], (N_A_LAYERS, D), f32),
        'g_norm_hgrn': 1.0 + 0.02 * nrm(ks[4], (N_A_LAYERS, HG_DV), f32),
        'w_out_hgrn': nrm(ks[5], (N_A_LAYERS, D, D), f32) * (D ** -0.5 * DEEPNORM_BETA),
        'w_kv': nrm(ks[6], (D, kv_width), f32) * D ** -0.5,
        'b_kv': 0.01 * nrm(ks[7], (kv_width,), f32),
        'w_q_attn': nrm(ks[8], (N_B_LAYERS, D, q_width), f32) * D ** -0.5,
        'b_q_attn': 0.01 * nrm(ks[9], (N_B_LAYERS, q_width), f32),
        'sinks': 0.5 * nrm(ks[10], (N_B_LAYERS, ATT_Q_HEADS), f32),
        'w_o_attn': nrm(ks[11], (N_B_LAYERS, q_width, D), f32) * (q_width ** -0.5 * DEEPNORM_BETA),
        'b_o_attn': 0.01 * nrm(ks[12], (N_B_LAYERS, D), f32),
        'w_router': nrm(ks[13], (D, N_EXPERTS), f32) * D ** -0.5,
        'router_bias': 0.01 * nrm(ks[14], (N_EXPERTS,), f32),
        'w_gate_up': nrm(ks[15], (DEPTH, N_EXPERTS, D, 2 * D_FF), f32) * D ** -0.5,
        'w_down': nrm(ks[16], (DEPTH, N_EXPERTS, D_FF, D), f32) * (D_FF ** -0.5 * DEEPNORM_BETA),
        'ln_mix_g': 1.0 + 0.02 * nrm(ks[17], (DEPTH, D), f32),
        'ln_mix_b': 0.02 * nrm(ks[18], (DEPTH, D), f32),
        'ln_ffn_g': 1.0 + 0.02 * nrm(ks[19], (DEPTH, D), f32),
        'ln_ffn_b': 0.02 * nrm(ks[20], (DEPTH, D), f32),
    }


def reference(x, positions, w_in_hgrn, lb_param, g_norm_hgrn, w_out_hgrn, w_kv, b_kv,
              w_q_attn, b_q_attn, sinks, w_o_attn, b_o_attn, w_router, router_bias,
              w_gate_up, w_down, ln_mix_g, ln_mix_b, ln_ffn_g, ln_ffn_b):
    lb_sm = jax.nn.softmax(lb_param.astype(jnp.float32), axis=0)
    lower_bounds = jnp.cumsum(lb_sm, axis=0) - lb_sm[0]
    k_shared = None
    v_shared = None
    for layer in range(DEPTH):
        if layer < N_A_LAYERS:
            mix = hgrn2_mixer(x, w_in_hgrn[layer], lower_bounds[layer], g_norm_hgrn[layer],
                              w_out_hgrn[layer])
        else:
            if layer == N_A_LAYERS:
                k_shared, v_shared = shared_kv(x, w_kv, b_kv, positions)
            j = layer - N_A_LAYERS
            mix = swa_mixer(x, k_shared, v_shared, w_q_attn[j], b_q_attn[j], sinks[j],
                            w_o_attn[j], b_o_attn[j], positions)
        x = layer_norm(DEEPNORM_ALPHA * x + mix, ln_mix_g[layer], ln_mix_b[layer])
        ffn = moe_ffn(x, w_router, router_bias, w_gate_up[layer], w_down[layer])
        x = layer_norm(DEEPNORM_ALPHA * x + ffn, ln_ffn_g[layer], ln_ffn_b[layer])
    return x
```

```python
import functools
import math

import numpy as np
import jax
import jax.numpy as jnp
from jax import lax
from jax.experimental import pallas as pl
from jax.experimental.pallas import tpu as pltpu

D_MODEL = 2048
BATCH = 4
SEQ = 2048
DEPTH = 4
N_TOK = BATCH * SEQ

N_A_LAYERS = DEPTH // 2
N_B_LAYERS = DEPTH - N_A_LAYERS
HG_EXPAND = 128
HG_HEADS = D_MODEL // HG_EXPAND
HG_DK = HG_EXPAND
HG_DV = D_MODEL // HG_HEADS
LB_FLOOR = 1e-30
ATT_HEAD_DIM = 64
ATT_Q_HEADS = D_MODEL // ATT_HEAD_DIM
ATT_KV_HEADS = ATT_Q_HEADS // 8
ATT_GROUP = ATT_Q_HEADS // ATT_KV_HEADS
WINDOW = 128
ROT_DIM = ATT_HEAD_DIM // 4
ROPE_THETA = 500000.0
MASK_VALUE = -1e30
N_EXPERTS = 32
N_GROUPS = 8
EXPERTS_PER_GROUP = N_EXPERTS // N_GROUPS
TOP_K = 2
D_FF = (704 * D_MODEL) // 2048
ROUTE_MASK = -1e9
DEEPNORM_ALPHA = (2 * DEPTH) ** 0.25
LN_EPS = 1e-5
RMS_EPS = 1e-6

LANES = 128
SCAN_CHUNK = 64
SCAN_LEVELS = 6
MOE_ROWS = 256
MOE_BLOCKS = (N_TOK * TOP_K + N_EXPERTS * (MOE_ROWS - 1) + MOE_ROWS - 1) // MOE_ROWS
MOE_PAD_ROWS = MOE_BLOCKS * MOE_ROWS
VMEM_LIMIT = 56 * 1024 * 1024

F32 = jnp.float32
BF16 = jnp.bfloat16


def _dot(a, b):
    return jnp.dot(a, b, preferred_element_type=F32)


def _dot_nt(a, b):
    return lax.dot_general(a, b, (((1,), (1,)), ((), ())), preferred_element_type=F32)


def _dot_tn(a, b):
    return lax.dot_general(a, b, (((0,), (0,)), ((), ())), preferred_element_type=F32)


def _sigmoid(x):
    return 1.0 / (1.0 + jnp.exp(-x))


def _params(sem, limit=VMEM_LIMIT):
    return pltpu.CompilerParams(dimension_semantics=sem, vmem_limit_bytes=limit)


def _matmul_kernel(a_ref, w_ref, o_ref):
    o_ref[...] = _dot(a_ref[...], w_ref[...]).astype(o_ref.dtype)


def _matmul(a, w, tm, tn, out_dtype):
    m, k = a.shape
    n = w.shape[1]
    return pl.pallas_call(
        _matmul_kernel,
        grid=(m // tm, n // tn),
        in_specs=[pl.BlockSpec((tm, k), lambda i, j: (i, 0)),
                  pl.BlockSpec((k, tn), lambda i, j: (0, j))],
        out_specs=pl.BlockSpec((tm, tn), lambda i, j: (i, j)),
        out_shape=jax.ShapeDtypeStruct((m, n), out_dtype),
        compiler_params=_params(("parallel", "parallel")),
        name="proj_matmul",
    )(a, w)


def _scan_constants():
    c = SCAN_CHUNK
    t = np.arange(c)[None, :]
    i = np.arange(c)[:, None]
    blocks = [(t <= i), (t > i)]
    masks = [np.eye(c, dtype=np.float32)]
    j = np.arange(c)[None, :]
    for lvl in range(1, SCAN_LEVELS + 1):
        s = 1 << lvl
        h = s // 2
        m = (i // s) * s + h - 1
        query_role = (i % s) >= h
        blocks.append(np.where(query_role, (t > m) & (t <= i), (t > i) & (t <= m)))
        masks.append((((i // s) == (j // s)) & query_role & ((j % s) < h)).astype(np.float32))
    a = np.concatenate(blocks, axis=0).astype(np.float32)
    return np.concatenate([a, a, a], axis=1), np.stack(masks)


def _hgrn_scan_kernel(q_ref, f_ref, v_ref, g_ref, lbp_ref, gn_ref, sums_ref, mask_ref, o_ref, st_ref, *, layer):
    c = SCAN_CHUNK
    lbp = lbp_ref[...]
    e = jnp.exp(lbp - jnp.max(lbp, axis=0, keepdims=True))
    sm = e / jnp.sum(e, axis=0, keepdims=True)
    lb = jnp.zeros((1, LANES), F32)
    for r in range(1, layer + 1):
        lb = lb + sm[r:r + 1, :]
    log_lb = jnp.log(jnp.maximum(lb, LB_FLOOR))
    log_1m_lb = jnp.log1p(-lb)
    one_m_lb = 1.0 - lb
    gn = gn_ref[...]
    st_ref[...] = jnp.zeros_like(st_ref)

    def chunk(ci, carry):
        r0 = pl.multiple_of(ci * c, c)
        q = q_ref[pl.ds(r0, c), :]
        f = f_ref[pl.ds(r0, c), :]
        v = v_ref[pl.ds(r0, c), :].astype(BF16)
        g = g_ref[pl.ds(r0, c), :]
        qs = q * _sigmoid(q) * (HG_DK ** -0.5)
        log_sig = jnp.minimum(f, 0.0) - jnp.log1p(jnp.exp(-jnp.abs(f)))
        a1 = log_lb
        a2 = log_1m_lb + log_sig
        log_f = jnp.maximum(a1, a2) + jnp.log1p(jnp.exp(-jnp.abs(a1 - a2)))
        k = one_m_lb * _sigmoid(-f)

        hi = log_f.astype(BF16)
        r1 = log_f - hi.astype(F32)
        mid = r1.astype(BF16)
        lo = (r1 - mid.astype(F32)).astype(BF16)
        sums = _dot(sums_ref[...], jnp.concatenate([hi, mid, lo], axis=0))
        ex = jnp.exp(sums)
        eb = ex[0:c]
        ek = ex[c:2 * c]

        scores = _dot_nt(qs.astype(BF16), k.astype(BF16)) * mask_ref[0]
        for lvl in range(1, SCAN_LEVELS + 1):
            el = ex[(lvl + 1) * c:(lvl + 2) * c]
            scores = scores + _dot_nt((qs * el).astype(BF16), (k * el).astype(BF16)) * mask_ref[lvl]

        st = st_ref[...]
        o = _dot_nt((qs * eb).astype(BF16), st.astype(BF16)) + _dot(scores.astype(BF16), v)
        st_ref[...] = st * eb[c - 1:c, :] + _dot_tn(v, (k * ek).astype(BF16))

        on = o * lax.rsqrt(jnp.mean(o * o, axis=-1, keepdims=True) + RMS_EPS)
        o_ref[pl.ds(r0, c), :] = (on * gn * (g * _sigmoid(g))).astype(o_ref.dtype)
        return carry

    lax.fori_loop(0, SEQ // c, chunk, 0)


def _hgrn_scan(proj, lb_param, g_norm, layer):
    sums_np, masks_np = _scan_constants()
    sums_m = jnp.asarray(sums_np, BF16)
    masks = jnp.asarray(masks_np, F32)
    h = HG_HEADS
    col = lambda off: pl.BlockSpec((SEQ, LANES), lambda b, hd: (b, hd + off * h))
    return pl.pallas_call(
        functools.partial(_hgrn_scan_kernel, layer=layer),
        grid=(BATCH, h),
        in_specs=[col(0), col(1), col(2), col(3),
                  pl.BlockSpec((N_A_LAYERS, LANES), lambda b, hd: (0, hd)),
                  pl.BlockSpec((1, LANES), lambda b, hd: (0, 0)),
                  pl.BlockSpec(sums_m.shape, lambda b, hd: (0, 0)),
                  pl.BlockSpec(masks.shape, lambda b, hd: (0, 0, 0))],
        out_specs=pl.BlockSpec((SEQ, LANES), lambda b, hd: (b, hd)),
        out_shape=jax.ShapeDtypeStruct((N_TOK, D_MODEL), BF16),
        scratch_shapes=[pltpu.VMEM((HG_DV, HG_DK), F32)],
        compiler_params=_params(("parallel", "parallel")),
        name="hgrn_scan",
    )(proj, proj, proj, proj, lb_param, g_norm.reshape(1, HG_DV), sums_m, masks)


def _layer_norm(y, g, b):
    mu = jnp.mean(y, axis=-1, keepdims=True)
    d = y - mu
    var = jnp.mean(d * d, axis=-1, keepdims=True)
    return d * lax.rsqrt(var + LN_EPS) * g + b


def _matmul_ln_kernel(a_ref, w_ref, bias_ref, res_ref, g_ref, b_ref, of_ref, ob_ref):
    mix = _dot(a_ref[...], w_ref[...]) + bias_ref[...]
    out = _layer_norm(DEEPNORM_ALPHA * res_ref[...] + mix, g_ref[...], b_ref[...])
    of_ref[...] = out
    ob_ref[...] = out.astype(BF16)


def _matmul_ln(a, w, bias, res, g, b, tm=256):
    m, k = a.shape
    n = w.shape[1]
    row = lambda width: pl.BlockSpec((1, width), lambda i: (0, 0))
    return pl.pallas_call(
        _matmul_ln_kernel,
        grid=(m // tm,),
        in_specs=[pl.BlockSpec((tm, k), lambda i: (i, 0)),
                  pl.BlockSpec((k, n), lambda i: (0, 0)),
                  row(n),
                  pl.BlockSpec((tm, n), lambda i: (i, 0)),
                  row(n), row(n)],
        out_specs=[pl.BlockSpec((tm, n), lambda i: (i, 0)), pl.BlockSpec((tm, n), lambda i: (i, 0))],
        out_shape=[jax.ShapeDtypeStruct((m, n), F32), jax.ShapeDtypeStruct((m, n), BF16)],
        compiler_params=_params(("parallel",)),
        name="out_proj_ln",
    )(a, w, bias.reshape(1, n), res, g.reshape(1, n), b.reshape(1, n))


def _proj_rotary_kernel(a_ref, w_ref, bias_ref, pos_ref, invf_ref, o_ref, *, rot_chunks, scale):
    acc = _dot(a_ref[...], w_ref[...]) + bias_ref[...]
    ang = pos_ref[...] * invf_ref[...]
    lane = lax.broadcasted_iota(jnp.int32, (1, LANES), 1) % ATT_HEAD_DIM
    half = ROT_DIM // 2
    cos = jnp.cos(ang)
    sin = jnp.sin(ang)
    c_keep = jnp.where(lane < ROT_DIM, cos, 1.0)
    s_from_left = jnp.where((lane >= half) & (lane < ROT_DIM), sin, 0.0)
    s_from_right = jnp.where(lane < half, -sin, 0.0)
    for j in range(acc.shape[1] // LANES):
        t = acc[:, j * LANES:(j + 1) * LANES]
        if j < rot_chunks:
            t = (t * c_keep + pltpu.roll(t, half, 1) * s_from_left
                 + pltpu.roll(t, LANES - half, 1) * s_from_right)
        o_ref[:, j * LANES:(j + 1) * LANES] = (t * scale).astype(o_ref.dtype)


def _proj_rotary(a, w, bias, pos, invf, rot_chunks, scale, tm=512, tn=512):
    m, k = a.shape
    n = w.shape[1]
    return pl.pallas_call(
        functools.partial(_proj_rotary_kernel, rot_chunks=rot_chunks, scale=scale),
        grid=(m // tm, n // tn),
        in_specs=[pl.BlockSpec((tm, k), lambda i, j: (i, 0)),
                  pl.BlockSpec((k, tn), lambda i, j: (0, j)),
                  pl.BlockSpec((1, tn), lambda i, j: (0, j)),
                  pl.BlockSpec((tm, 1), lambda i, j: (i, 0)),
                  pl.BlockSpec((1, LANES), lambda i, j: (0, 0))],
        out_specs=pl.BlockSpec((tm, tn), lambda i, j: (i, j)),
        out_shape=jax.ShapeDtypeStruct((m, n), BF16),
        compiler_params=_params(("parallel", "parallel")),
        name="proj_rotary",
    )(a, w, bias.reshape(1, n), pos, invf)


def _attn_kernel(sink_ref, q_ref, kp_ref, kc_ref, vp_ref, vc_ref, o_ref):
    nb = pl.program_id(1)
    w = WINDOW
    hd = ATT_HEAD_DIM
    k = jnp.concatenate([kp_ref[...], kc_ref[...]], axis=0)
    v = jnp.concatenate([vp_ref[...], vc_ref[...]], axis=0)
    qi = lax.broadcasted_iota(jnp.int32, (w, 2 * w), 0) + w
    ki = lax.broadcasted_iota(jnp.int32, (w, 2 * w), 1)
    rel = qi - ki
    valid = (rel >= 0) & (rel < w) & ((ki >= w) | (nb > 0))
    for j in range(ATT_KV_HEADS):
        kj = k[:, j * hd:(j + 1) * hd]
        vj = v[:, j * hd:(j + 1) * hd]
        heads = [j * ATT_GROUP + gq for gq in range(ATT_GROUP)]
        qw = q_ref[:, j * ATT_GROUP * hd:(j + 1) * ATT_GROUP * hd]
        qj = jnp.concatenate([qw[:, gq * hd:(gq + 1) * hd] for gq in range(ATT_GROUP)], axis=0)
        s_all = _dot_nt(qj, kj)
        outs = []
        for gq, h in enumerate(heads):
            s = jnp.where(valid, s_all[gq * w:(gq + 1) * w], MASK_VALUE)
            sink = sink_ref[h]
            m = jnp.maximum(jnp.max(s, axis=-1, keepdims=True), sink)
            p = jnp.exp(s - m)
            denom = jnp.sum(p, axis=-1, keepdims=True) + jnp.exp(sink - m)
            outs.append(_dot(p.astype(BF16), vj) / denom)
        for gq in range(0, ATT_GROUP, 2):
            h = heads[gq]
            o_ref[:, h * hd:(h + 2) * hd] = jnp.concatenate(outs[gq:gq + 2], axis=1).astype(o_ref.dtype)


def _attention(q, kv, sinks):
    nblk = SEQ // WINDOW
    kvw = ATT_KV_HEADS * ATT_HEAD_DIM
    cur = lambda col: pl.BlockSpec((WINDOW, kvw), lambda b, n, s: (b * nblk + n, col))
    prev = lambda col: pl.BlockSpec((WINDOW, kvw), lambda b, n, s: (b * nblk + jnp.maximum(n - 1, 0), col))
    grid_spec = pltpu.PrefetchScalarGridSpec(
        num_scalar_prefetch=1,
        grid=(BATCH, nblk),
        in_specs=[pl.BlockSpec((WINDOW, D_MODEL), lambda b, n, s: (b * nblk + n, 0)),
                  prev(0), cur(0), prev(1), cur(1)],
        out_specs=pl.BlockSpec((WINDOW, D_MODEL), lambda b, n, s: (b * nblk + n, 0)),
    )
    return pl.pallas_call(
        _attn_kernel,
        grid_spec=grid_spec,
        out_shape=jax.ShapeDtypeStruct((N_TOK, D_MODEL), BF16),
        compiler_params=_params(("parallel", "parallel")),
        name="swa_attention",
    )(sinks, q, kv, kv, kv, kv)


def _router_kernel(x_ref, wh_ref, wl_ref, bias_ref, idx_ref, gate_ref):
    x = x_ref[...]
    xh = x.astype(BF16)
    xl = (x - xh.astype(F32)).astype(BF16)
    wh = wh_ref[...]
    logits = _dot(xh, wh) + _dot(xl, wh) + _dot(xh, wl_ref[...])
    lt = logits.T[0:N_EXPERTS]
    scores = _sigmoid(lt)
    biased = scores + bias_ref[...]
    g = N_GROUPS
    a, b, c, d = (biased[p * g:(p + 1) * g] for p in range(EXPERTS_PER_GROUP))
    group_score = jnp.maximum(jnp.maximum(a + b, c + d), jnp.maximum(a, b) + jnp.maximum(c, d))
    gi = lax.broadcasted_iota(jnp.int32, group_score.shape, 0)
    best = jnp.min(jnp.where(group_score == jnp.max(group_score, axis=0, keepdims=True), gi, g),
                   axis=0, keepdims=True)
    row = lax.broadcasted_iota(jnp.int32, biased.shape, 0)
    expert = (row % g) * EXPERTS_PER_GROUP + row // g
    masked = jnp.where((row % g) == best, biased, ROUTE_MASK)

    def take_top(vals):
        top = jnp.max(vals, axis=0, keepdims=True)
        return jnp.min(jnp.where(vals == top, expert, N_EXPERTS), axis=0, keepdims=True)

    first = take_top(masked)
    second = take_top(jnp.where(expert == first, -jnp.inf, masked))
    w1 = jnp.sum(jnp.where(expert == first, scores, 0.0), axis=0, keepdims=True)
    w2 = jnp.sum(jnp.where(expert == second, scores, 0.0), axis=0, keepdims=True)
    idx_ref[0:1, :] = first
    idx_ref[1:2, :] = second
    gate_ref[0:1, :] = w1 / (w1 + w2)
    gate_ref[1:2, :] = w2 / (w1 + w2)


def _router(x, wh, wl, bias, tm=512):
    m, k = x.shape
    return pl.pallas_call(
        _router_kernel,
        grid=(m // tm,),
        in_specs=[pl.BlockSpec((tm, k), lambda i: (i, 0)),
                  pl.BlockSpec((k, LANES), lambda i: (0, 0)),
                  pl.BlockSpec((k, LANES), lambda i: (0, 0)),
                  pl.BlockSpec((N_EXPERTS, 1), lambda i: (0, 0))],
        out_specs=[pl.BlockSpec((TOP_K, tm), lambda i: (0, i)), pl.BlockSpec((TOP_K, tm), lambda i: (0, i))],
        out_shape=[jax.ShapeDtypeStruct((TOP_K, m), jnp.int32), jax.ShapeDtypeStruct((TOP_K, m), F32)],
        compiler_params=_params(("parallel",)),
        name="router",
    )(x, wh, wl, bias)


def _expert_kernel(be_ref, nused_ref, x_ref, wgu_ref, wd_ref, y_ref, wgu_s, wd_s):
    i = pl.program_id(0)
    e = be_ref[i]
    prev = be_ref[jnp.maximum(i - 1, 0)]

    @pl.when((i == 0) | (e != prev))
    def _():
        wgu_s[...] = wgu_ref[...].astype(BF16)
        wd_s[...] = wd_ref[...].astype(BF16)

    @pl.when(i < nused_ref[0])
    def _():
        h = _dot(x_ref[...], wgu_s[...])
        hg = h[:, :D_FF]
        hu = h[:, D_FF:]
        act = (hg * _sigmoid(hg) * hu).astype(BF16)
        y_ref[...] = _dot(act, wd_s[...]).astype(y_ref.dtype)

    @pl.when(i >= nused_ref[0])
    def _():
        y_ref[...] = jnp.zeros_like(y_ref)


def _expert_ffn(x_pad, block_expert, n_used, w_gate_up, w_down, layer):
    grid_spec = pltpu.PrefetchScalarGridSpec(
        num_scalar_prefetch=2,
        grid=(MOE_BLOCKS,),
        in_specs=[pl.BlockSpec((MOE_ROWS, D_MODEL), lambda i, be, nu: (i, 0)),
                  pl.BlockSpec((None, None, D_MODEL, 2 * D_FF), lambda i, be, nu: (layer, be[i], 0, 0)),
                  pl.BlockSpec((None, None, D_FF, D_MODEL), lambda i, be, nu: (layer, be[i], 0, 0))],
        out_specs=pl.BlockSpec((MOE_ROWS, D_MODEL), lambda i, be, nu: (i, 0)),
        scratch_shapes=[pltpu.VMEM((D_MODEL, 2 * D_FF), BF16), pltpu.VMEM((D_FF, D_MODEL), BF16)],
    )
    return pl.pallas_call(
        _expert_kernel,
        grid_spec=grid_spec,
        out_shape=jax.ShapeDtypeStruct((MOE_PAD_ROWS, D_MODEL), F32),
        compiler_params=_params(("arbitrary",), 58 * 1024 * 1024),
        name="expert_ffn",
    )(block_expert, n_used, x_pad, w_gate_up, w_down)


def _combine_ln_kernel(res_ref, y0_ref, y1_ref, gate_ref, g_ref, b_ref, of_ref, ob_ref):
    gate = gate_ref[...]
    ffn = y0_ref[...] * gate[:, 0:1] + y1_ref[...] * gate[:, 1:2]
    out = _layer_norm(DEEPNORM_ALPHA * res_ref[...] + ffn, g_ref[...], b_ref[...])
    of_ref[...] = out
    ob_ref[...] = out.astype(BF16)


def _combine_ln(res, y0, y1, gate, g, b, tm=256):
    m, n = res.shape
    blk = pl.BlockSpec((tm, n), lambda i: (i, 0))
    row = pl.BlockSpec((1, n), lambda i: (0, 0))
    return pl.pallas_call(
        _combine_ln_kernel,
        grid=(m // tm,),
        in_specs=[blk, blk, blk, pl.BlockSpec((tm, TOP_K), lambda i: (i, 0)), row, row],
        out_specs=[blk, blk],
        out_shape=[jax.ShapeDtypeStruct((m, n), F32), jax.ShapeDtypeStruct((m, n), BF16)],
        compiler_params=_params(("parallel",)),
        name="combine_ln",
    )(res, y0, y1, gate, g.reshape(1, n), b.reshape(1, n))


def _dispatch_tables(idx):
    e_flat = idx.T.reshape(-1)
    onehot = (e_flat[:, None] == jnp.arange(N_EXPERTS, dtype=jnp.int32)[None, :]).astype(jnp.int32)
    csum = jnp.cumsum(onehot, axis=0)
    rank = jnp.take_along_axis(csum, e_flat[:, None], axis=1)[:, 0] - 1
    counts = csum[-1]
    padded = (counts + MOE_ROWS - 1) // MOE_ROWS * MOE_ROWS
    pad_end = jnp.cumsum(padded)
    dest = (pad_end - padded)[e_flat] + rank
    tok = jnp.arange(N_TOK * TOP_K, dtype=jnp.int32) // TOP_K
    src_tok = jnp.zeros((MOE_PAD_ROWS,), jnp.int32).at[dest].set(tok)
    starts = jnp.arange(MOE_BLOCKS, dtype=jnp.int32) * MOE_ROWS
    block_expert = jnp.minimum(jnp.searchsorted(pad_end, starts, side='right'), N_EXPERTS - 1).astype(jnp.int32)
    n_used = (pad_end[-1:] // MOE_ROWS).astype(jnp.int32)
    return dest.reshape(N_TOK, TOP_K), src_tok, block_expert, n_used


def _moe_layer(xf, xb, router_w, w_gate_up, w_down, ln_g, ln_b, layer):
    wh, wl, bias = router_w
    idx, gate = _router(xf, wh, wl, bias)
    dest, src_tok, block_expert, n_used = _dispatch_tables(idx)
    x_pad = jnp.take(xb, src_tok, axis=0)
    y_pad = _expert_ffn(x_pad, block_expert, n_used, w_gate_up, w_down, layer)
    y0 = jnp.take(y_pad, dest[:, 0], axis=0)
    y1 = jnp.take(y_pad, dest[:, 1], axis=0)
    return _combine_ln(xf, y0, y1, gate.T, ln_g, ln_b)


def _router_weights(w_router, router_bias):
    lane = np.arange(N_EXPERTS)
    perm = (lane % N_GROUPS) * EXPERTS_PER_GROUP + lane // N_GROUPS
    w = jnp.pad(w_router[:, perm], ((0, 0), (0, LANES - N_EXPERTS)))
    wh = w.astype(BF16)
    wl = (w - wh.astype(F32)).astype(BF16)
    return wh, wl, router_bias[perm].astype(F32).reshape(N_EXPERTS, 1)


def kernel(x, positions, w_in_hgrn, lb_param, g_norm_hgrn, w_out_hgrn, w_kv, b_kv, w_q_attn, b_q_attn, sinks, w_o_attn, b_o_attn, w_router, router_bias, w_gate_up, w_down, ln_mix_g, ln_mix_b, ln_ffn_g, ln_ffn_b):
    xf = x.reshape(N_TOK, D_MODEL)
    xb = xf.astype(BF16)
    pos = positions.reshape(N_TOK, 1).astype(F32)
    half = ROT_DIM // 2
    inv_freq = 1.0 / (ROPE_THETA ** (jnp.arange(half, dtype=F32) * 2.0 / ROT_DIM))
    lane = np.arange(LANES) % ATT_HEAD_DIM
    invf = jnp.where(lane < ROT_DIM, inv_freq[lane % half], 0.0).reshape(1, LANES).astype(F32)
    router_w = _router_weights(w_router, router_bias)
    zero_bias = jnp.zeros((D_MODEL,), F32)
    kv = None
    for layer in range(DEPTH):
        if layer < N_A_LAYERS:
            proj = _matmul(xb, w_in_hgrn[layer].astype(BF16), 512, 1024, F32)
            o = _hgrn_scan(proj, lb_param.astype(F32), g_norm_hgrn[layer], layer)
            xf, xb = _matmul_ln(o, w_out_hgrn[layer].astype(BF16), zero_bias, xf,
                                ln_mix_g[layer], ln_mix_b[layer])
        else:
            if layer == N_A_LAYERS:
                kv = _proj_rotary(xb, w_kv.astype(BF16), b_kv, pos, invf,
                                  rot_chunks=ATT_KV_HEADS * ATT_HEAD_DIM // LANES, scale=1.0)
            j = layer - N_A_LAYERS
            q = _proj_rotary(xb, w_q_attn[j].astype(BF16), b_q_attn[j], pos, invf,
                             rot_chunks=D_MODEL // LANES, scale=ATT_HEAD_DIM ** -0.5)
            o = _attention(q, kv, sinks[j].astype(F32))
            xf, xb = _matmul_ln(o, w_o_attn[j].astype(BF16), b_o_attn[j], xf,
                                ln_mix_g[layer], ln_mix_b[layer])
        xf, xb = _moe_layer(xf, xb, router_w, w_gate_up, w_down, ln_ffn_g[layer], ln_ffn_b[layer], layer)
    return xf.reshape(BATCH, SEQ, D_MODEL)
```

```python
import functools
import math

import numpy as np
import jax
import jax.numpy as jnp
from jax import lax
from jax.experimental import pallas as pl
from jax.experimental.pallas import tpu as pltpu

D_MODEL = 2048
BATCH = 4
SEQ = 2048
DEPTH = 4
N_TOK = BATCH * SEQ

N_A_LAYERS = DEPTH // 2
N_B_LAYERS = DEPTH - N_A_LAYERS
HG_EXPAND = 128
HG_HEADS = D_MODEL // HG_EXPAND
HG_DK = HG_EXPAND
HG_DV = D_MODEL // HG_HEADS
LB_FLOOR = 1e-30
ATT_HEAD_DIM = 64
ATT_Q_HEADS = D_MODEL // ATT_HEAD_DIM
ATT_KV_HEADS = ATT_Q_HEADS // 8
ATT_GROUP = ATT_Q_HEADS // ATT_KV_HEADS
WINDOW = 128
ROT_DIM = ATT_HEAD_DIM // 4
ROPE_THETA = 500000.0
MASK_VALUE = -1e30
N_EXPERTS = 32
N_GROUPS = 8
EXPERTS_PER_GROUP = N_EXPERTS // N_GROUPS
TOP_K = 2
D_FF = (704 * D_MODEL) // 2048
ROUTE_MASK = -1e9
DEEPNORM_ALPHA = (2 * DEPTH) ** 0.25
LN_EPS = 1e-5
RMS_EPS = 1e-6

LANES = 128
SCAN_CHUNK = 64
SCAN_LEVELS = 6
SCAN_HEADS = 4
SCAN_ROWS = 16
MOE_ROWS = 256
MOE_BLOCKS = (N_TOK * TOP_K + N_EXPERTS * (MOE_ROWS - 1) + MOE_ROWS - 1) // MOE_ROWS
MOE_PAD_ROWS = MOE_BLOCKS * MOE_ROWS
VMEM_LIMIT = 56 * 1024 * 1024

F32 = jnp.float32
BF16 = jnp.bfloat16


def _dot(a, b):
    return jnp.dot(a, b, preferred_element_type=F32)


def _dot_nt(a, b):
    return lax.dot_general(a, b, (((1,), (1,)), ((), ())), preferred_element_type=F32)


def _dot_tn(a, b):
    return lax.dot_general(a, b, (((0,), (0,)), ((), ())), preferred_element_type=F32)


def _sigmoid(x):
    return 1.0 / (1.0 + jnp.exp(-x))


def _params(sem, limit=VMEM_LIMIT):
    return pltpu.CompilerParams(dimension_semantics=sem, vmem_limit_bytes=limit)


def _hgrn_in_kernel(a_ref, w_ref, lbp_ref, *o_refs, part, layer):
    acc = _dot(a_ref[...], w_ref[...])
    if part == "q":
        o_refs[0][...] = (acc * _sigmoid(acc) * (HG_DK ** -0.5)).astype(BF16)
    elif part == "v":
        o_refs[0][...] = acc.astype(BF16)
    elif part == "g":
        o_refs[0][...] = (acc * _sigmoid(acc)).astype(BF16)
    else:
        lbp = lbp_ref[...]
        e = jnp.exp(lbp - jnp.max(lbp, axis=0, keepdims=True))
        sm = e / jnp.sum(e, axis=0, keepdims=True)
        lb = jnp.zeros_like(sm[0:1])
        for r in range(1, layer + 1):
            lb = lb + sm[r:r + 1]
        ez = jnp.exp(-jnp.abs(acc))
        rz = 1.0 / (1.0 + ez)
        sig_pos = jnp.where(acc >= 0, rz, ez * rz)
        sig_neg = jnp.where(acc >= 0, ez * rz, rz)
        forget = jnp.maximum(lb, LB_FLOOR) + (1.0 - lb) * sig_pos
        o_refs[0][...] = jnp.log(forget) * (1.0 / math.log(2.0))
        o_refs[1][...] = ((1.0 - lb) * sig_neg).astype(BF16)


def _hgrn_in_proj(a, w, lb_param, part, layer, tm=512, tn=1024):
    m, k = a.shape
    n = w.shape[1]
    blk = pl.BlockSpec((tm, tn), lambda i, j: (i, j))
    if part == "f":
        out_specs = [blk, blk]
        out_shape = [jax.ShapeDtypeStruct((m, n), F32), jax.ShapeDtypeStruct((m, n), BF16)]
    else:
        out_specs = [blk]
        out_shape = [jax.ShapeDtypeStruct((m, n), BF16)]
    return pl.pallas_call(
        functools.partial(_hgrn_in_kernel, part=part, layer=layer),
        grid=(m // tm, n // tn),
        in_specs=[pl.BlockSpec((tm, k), lambda i, j: (i, 0)),
                  pl.BlockSpec((k, tn), lambda i, j: (0, j)),
                  pl.BlockSpec((N_A_LAYERS, tn), lambda i, j: (0, j))],
        out_specs=out_specs,
        out_shape=out_shape,
        compiler_params=_params(("parallel", "parallel")),
        name="hgrn_in_" + part,
    )(a, w, lb_param)


def _scan_constants():
    c = SCAN_CHUNK
    t = np.arange(c)[None, :]
    i = np.arange(c)[:, None]
    blocks = [(t <= i), (t > i)]
    masks = [np.eye(c, dtype=np.float32)]
    j = np.arange(c)[None, :]
    for lvl in range(1, SCAN_LEVELS + 1):
        s = 1 << lvl
        h = s // 2
        m = (i // s) * s + h - 1
        query_role = (i % s) >= h
        blocks.append(np.where(query_role, (t > m) & (t <= i), (t > i) & (t <= m)))
        masks.append((((i // s) == (j // s)) & query_role & ((j % s) < h)).astype(np.float32))
    a = np.concatenate(blocks, axis=0).astype(np.float32)
    return np.concatenate([a, a], axis=1), np.stack(masks)


def _query_row_blocks(lvl):
    size = 1 << lvl
    nblk = SCAN_CHUNK // SCAN_ROWS
    if size <= SCAN_ROWS:
        return list(range(nblk))
    return [r for r in range(nblk) if (r * SCAN_ROWS) % size >= size // 2]


def _hgrn_scan_kernel(qs_ref, k_ref, l2f_ref, v_ref, gg_ref, gn_ref, sums_ref, mask_ref, o_ref, st_ref, ex_ref):
    c = SCAN_CHUNK
    rb = SCAN_ROWS
    nblk = c // rb
    gn = gn_ref[...]
    st_ref[...] = jnp.zeros_like(st_ref)

    def decay_stage(r0, buf):
        for pair in range(SCAN_HEADS // 2):
            cols = slice(pair * 2 * LANES, (pair + 1) * 2 * LANES)
            l2f = l2f_ref[pl.ds(r0, c), cols]
            hi = l2f.astype(BF16)
            lo = (l2f - hi.astype(F32)).astype(BF16)
            ex_ref[buf, :, cols] = jnp.exp2(_dot(sums_ref[...], jnp.concatenate([hi, lo], axis=0)))

    every = list(range(nblk))

    def decays(buf, hd, block, rows):
        cols = slice(hd * LANES, (hd + 1) * LANES)
        return jnp.concatenate(
            [ex_ref[buf, block * c + r * rb:block * c + (r + 1) * rb, cols] for r in rows], axis=0)

    def intra_scores(r0, hd, buf):
        cols = slice(hd * LANES, (hd + 1) * LANES)
        qs = qs_ref[pl.ds(r0, c), cols]
        k = k_ref[pl.ds(r0, c), cols]
        qs32 = qs.astype(F32)
        k32 = k.astype(F32)
        diag = _dot_nt(qs, k)
        sc = [diag[r * rb:(r + 1) * rb] * mask_ref[0, r * rb:(r + 1) * rb, :] for r in every]
        for lvl in range(1, SCAN_LEVELS + 1):
            rows = _query_row_blocks(lvl)
            ql = jnp.concatenate([qs32[r * rb:(r + 1) * rb] for r in rows], axis=0) * decays(buf, hd, lvl + 1, rows)
            part = _dot_nt(ql.astype(BF16), (k32 * decays(buf, hd, lvl + 1, every)).astype(BF16))
            for n, r in enumerate(rows):
                sc[r] = sc[r] + part[n * rb:(n + 1) * rb] * mask_ref[lvl, r * rb:(r + 1) * rb, :]
        return jnp.concatenate(sc, axis=0).astype(BF16)

    def state_step(r0, hd, buf, scores):
        cols = slice(hd * LANES, (hd + 1) * LANES)
        qs32 = qs_ref[pl.ds(r0, c), cols].astype(F32)
        k32 = k_ref[pl.ds(r0, c), cols].astype(F32)
        v = v_ref[pl.ds(r0, c), cols]
        eb = decays(buf, hd, 0, every)
        ek = decays(buf, hd, 1, every)
        st = st_ref[hd]
        o = _dot_nt((qs32 * eb).astype(BF16), st.astype(BF16)) + _dot(scores, v)
        st_ref[hd] = st * eb[c - 1:c, :] + _dot_tn(v, (k32 * ek).astype(BF16))
        on = o * lax.rsqrt(jnp.mean(o * o, axis=-1, keepdims=True) + RMS_EPS)
        o_ref[pl.ds(r0, c), cols] = (on * gn * gg_ref[pl.ds(r0, c), cols].astype(F32)).astype(o_ref.dtype)

    decay_stage(0, 0)

    def chunk_pair(cp, carry):
        for half in range(2):
            r0 = pl.multiple_of(cp * (2 * c) + half * c, c)
            scores = [intra_scores(r0, hd, half) for hd in range(SCAN_HEADS)]
            decay_stage(pl.multiple_of(jnp.minimum(r0 + c, SEQ - c), c), 1 - half)
            for hd in range(SCAN_HEADS):
                state_step(r0, hd, half, scores[hd])
        return carry

    lax.fori_loop(0, SEQ // (2 * c), chunk_pair, 0)


def _hgrn_scan(qs, k, l2f, v, gg, g_norm):
    sums_np, masks_np = _scan_constants()
    sums_m = jnp.asarray(sums_np, BF16)
    masks = jnp.asarray(masks_np, F32)
    width = SCAN_HEADS * LANES
    blk = pl.BlockSpec((SEQ, width), lambda b, hg: (b, hg))
    return pl.pallas_call(
        _hgrn_scan_kernel,
        grid=(BATCH, HG_HEADS // SCAN_HEADS),
        in_specs=[blk, blk, blk, blk, blk,
                  pl.BlockSpec((1, LANES), lambda b, hg: (0, 0)),
                  pl.BlockSpec(sums_m.shape, lambda b, hg: (0, 0)),
                  pl.BlockSpec(masks.shape, lambda b, hg: (0, 0, 0))],
        out_specs=blk,
        out_shape=jax.ShapeDtypeStruct((N_TOK, D_MODEL), BF16),
        scratch_shapes=[pltpu.VMEM((SCAN_HEADS, HG_DV, HG_DK), F32),
                        pltpu.VMEM((2, (SCAN_LEVELS + 2) * SCAN_CHUNK, width), F32)],
        compiler_params=_params(("parallel", "parallel")),
        name="hgrn_scan",
    )(qs, k, l2f, v, gg, g_norm.reshape(1, HG_DV), sums_m, masks)


def _layer_norm(y, g, b):
    mu = jnp.mean(y, axis=-1, keepdims=True)
    d = y - mu
    var = jnp.mean(d * d, axis=-1, keepdims=True)
    return d * lax.rsqrt(var + LN_EPS) * g + b


def _matmul_ln_kernel(a_ref, w_ref, bias_ref, res_ref, g_ref, b_ref, of_ref, ob_ref):
    mix = _dot(a_ref[...], w_ref[...]) + bias_ref[...]
    out = _layer_norm(DEEPNORM_ALPHA * res_ref[...] + mix, g_ref[...], b_ref[...])
    of_ref[...] = out
    ob_ref[...] = out.astype(BF16)


def _matmul_ln(a, w, bias, res, g, b, tm=256):
    m, k = a.shape
    n = w.shape[1]
    row = lambda width: pl.BlockSpec((1, width), lambda i: (0, 0))
    return pl.pallas_call(
        _matmul_ln_kernel,
        grid=(m // tm,),
        in_specs=[pl.BlockSpec((tm, k), lambda i: (i, 0)),
                  pl.BlockSpec((k, n), lambda i: (0, 0)),
                  row(n),
                  pl.BlockSpec((tm, n), lambda i: (i, 0)),
                  row(n), row(n)],
        out_specs=[pl.BlockSpec((tm, n), lambda i: (i, 0)), pl.BlockSpec((tm, n), lambda i: (i, 0))],
        out_shape=[jax.ShapeDtypeStruct((m, n), F32), jax.ShapeDtypeStruct((m, n), BF16)],
        compiler_params=_params(("parallel",)),
        name="out_proj_ln",
    )(a, w, bias.reshape(1, n), res, g.reshape(1, n), b.reshape(1, n))


def _proj_rotary_kernel(a_ref, w_ref, bias_ref, pos_ref, invf_ref, o_ref, *, rot_chunks, scale):
    acc = _dot(a_ref[...], w_ref[...]) + bias_ref[...]
    ang = pos_ref[...] * invf_ref[...]
    lane = lax.broadcasted_iota(jnp.int32, (1, LANES), 1) % ATT_HEAD_DIM
    half = ROT_DIM // 2
    cos = jnp.cos(ang)
    sin = jnp.sin(ang)
    c_keep = jnp.where(lane < ROT_DIM, cos, 1.0)
    s_from_left = jnp.where((lane >= half) & (lane < ROT_DIM), sin, 0.0)
    s_from_right = jnp.where(lane < half, -sin, 0.0)
    for j in range(acc.shape[1] // LANES):
        t = acc[:, j * LANES:(j + 1) * LANES]
        if j < rot_chunks:
            t = (t * c_keep + pltpu.roll(t, half, 1) * s_from_left
                 + pltpu.roll(t, LANES - half, 1) * s_from_right)
        o_ref[:, j * LANES:(j + 1) * LANES] = (t * scale).astype(o_ref.dtype)


def _proj_rotary(a, w, bias, pos, invf, rot_chunks, scale, tm=512, tn=512):
    m, k = a.shape
    n = w.shape[1]
    return pl.pallas_call(
        functools.partial(_proj_rotary_kernel, rot_chunks=rot_chunks, scale=scale),
        grid=(m // tm, n // tn),
        in_specs=[pl.BlockSpec((tm, k), lambda i, j: (i, 0)),
                  pl.BlockSpec((k, tn), lambda i, j: (0, j)),
                  pl.BlockSpec((1, tn), lambda i, j: (0, j)),
                  pl.BlockSpec((tm, 1), lambda i, j: (i, 0)),
                  pl.BlockSpec((1, LANES), lambda i, j: (0, 0))],
        out_specs=pl.BlockSpec((tm, tn), lambda i, j: (i, j)),
        out_shape=jax.ShapeDtypeStruct((m, n), BF16),
        compiler_params=_params(("parallel", "parallel")),
        name="proj_rotary",
    )(a, w, bias.reshape(1, n), pos, invf)


def _attn_kernel(sink_ref, q_ref, kp_ref, kc_ref, vp_ref, vc_ref, o_ref):
    nb = pl.program_id(1)
    w = WINDOW
    hd = ATT_HEAD_DIM
    k = jnp.concatenate([kp_ref[...], kc_ref[...]], axis=0)
    v = jnp.concatenate([vp_ref[...], vc_ref[...]], axis=0)
    qi = lax.broadcasted_iota(jnp.int32, (w, 2 * w), 0) + w
    ki = lax.broadcasted_iota(jnp.int32, (w, 2 * w), 1)
    rel = qi - ki
    valid = (rel >= 0) & (rel < w) & ((ki >= w) | (nb > 0))
    for j in range(ATT_KV_HEADS):
        kj = k[:, j * hd:(j + 1) * hd]
        vj = v[:, j * hd:(j + 1) * hd]
        heads = [j * ATT_GROUP + gq for gq in range(ATT_GROUP)]
        qw = q_ref[:, j * ATT_GROUP * hd:(j + 1) * ATT_GROUP * hd]
        qj = jnp.concatenate([qw[:, gq * hd:(gq + 1) * hd] for gq in range(ATT_GROUP)], axis=0)
        s_all = _dot_nt(qj, kj)
        outs = []
        for gq, h in enumerate(heads):
            s = jnp.where(valid, s_all[gq * w:(gq + 1) * w], MASK_VALUE)
            sink = sink_ref[h]
            m = jnp.maximum(jnp.max(s, axis=-1, keepdims=True), sink)
            p = jnp.exp(s - m)
            denom = jnp.sum(p, axis=-1, keepdims=True) + jnp.exp(sink - m)
            outs.append(_dot(p.astype(BF16), vj) / denom)
        for gq in range(0, ATT_GROUP, 2):
            h = heads[gq]
            o_ref[:, h * hd:(h + 2) * hd] = jnp.concatenate(outs[gq:gq + 2], axis=1).astype(o_ref.dtype)


def _attention(q, kv, sinks):
    nblk = SEQ // WINDOW
    kvw = ATT_KV_HEADS * ATT_HEAD_DIM
    cur = lambda col: pl.BlockSpec((WINDOW, kvw), lambda b, n, s: (b * nblk + n, col))
    prev = lambda col: pl.BlockSpec((WINDOW, kvw), lambda b, n, s: (b * nblk + jnp.maximum(n - 1, 0), col))
    grid_spec = pltpu.PrefetchScalarGridSpec(
        num_scalar_prefetch=1,
        grid=(BATCH, nblk),
        in_specs=[pl.BlockSpec((WINDOW, D_MODEL), lambda b, n, s: (b * nblk + n, 0)),
                  prev(0), cur(0), prev(1), cur(1)],
        out_specs=pl.BlockSpec((WINDOW, D_MODEL), lambda b, n, s: (b * nblk + n, 0)),
    )
    return pl.pallas_call(
        _attn_kernel,
        grid_spec=grid_spec,
        out_shape=jax.ShapeDtypeStruct((N_TOK, D_MODEL), BF16),
        compiler_params=_params(("parallel", "parallel")),
        name="swa_attention",
    )(sinks, q, kv, kv, kv, kv)


def _router_kernel(x_ref, wh_ref, wl_ref, bias_ref, idx_ref, gate_ref):
    x = x_ref[...]
    xh = x.astype(BF16)
    xl = (x - xh.astype(F32)).astype(BF16)
    wh = wh_ref[...]
    logits = _dot(xh, wh) + _dot(xl, wh) + _dot(xh, wl_ref[...])
    lt = logits.T[0:N_EXPERTS]
    scores = _sigmoid(lt)
    biased = scores + bias_ref[...]
    g = N_GROUPS
    a, b, c, d = (biased[p * g:(p + 1) * g] for p in range(EXPERTS_PER_GROUP))
    group_score = jnp.maximum(jnp.maximum(a + b, c + d), jnp.maximum(a, b) + jnp.maximum(c, d))
    gi = lax.broadcasted_iota(jnp.int32, group_score.shape, 0)
    best = jnp.min(jnp.where(group_score == jnp.max(group_score, axis=0, keepdims=True), gi, g),
                   axis=0, keepdims=True)
    row = lax.broadcasted_iota(jnp.int32, biased.shape, 0)
    expert = (row % g) * EXPERTS_PER_GROUP + row // g
    masked = jnp.where((row % g) == best, biased, ROUTE_MASK)

    def take_top(vals):
        top = jnp.max(vals, axis=0, keepdims=True)
        return jnp.min(jnp.where(vals == top, expert, N_EXPERTS), axis=0, keepdims=True)

    first = take_top(masked)
    second = take_top(jnp.where(expert == first, -jnp.inf, masked))
    w1 = jnp.sum(jnp.where(expert == first, scores, 0.0), axis=0, keepdims=True)
    w2 = jnp.sum(jnp.where(expert == second, scores, 0.0), axis=0, keepdims=True)
    idx_ref[0:1, :] = first
    idx_ref[1:2, :] = second
    gate_ref[0:1, :] = w1 / (w1 + w2)
    gate_ref[1:2, :] = w2 / (w1 + w2)


def _router(x, wh, wl, bias, tm=512):
    m, k = x.shape
    return pl.pallas_call(
        _router_kernel,
        grid=(m // tm,),
        in_specs=[pl.BlockSpec((tm, k), lambda i: (i, 0)),
                  pl.BlockSpec((k, LANES), lambda i: (0, 0)),
                  pl.BlockSpec((k, LANES), lambda i: (0, 0)),
                  pl.BlockSpec((N_EXPERTS, 1), lambda i: (0, 0))],
        out_specs=[pl.BlockSpec((TOP_K, tm), lambda i: (0, i)), pl.BlockSpec((TOP_K, tm), lambda i: (0, i))],
        out_shape=[jax.ShapeDtypeStruct((TOP_K, m), jnp.int32), jax.ShapeDtypeStruct((TOP_K, m), F32)],
        compiler_params=_params(("parallel",)),
        name="router",
    )(x, wh, wl, bias)


def _expert_kernel(be_ref, nused_ref, x_ref, wgu_ref, wd_ref, y_ref, wgu_s, wd_s):
    i = pl.program_id(0)
    e = be_ref[i]
    prev = be_ref[jnp.maximum(i - 1, 0)]

    @pl.when((i == 0) | (e != prev))
    def _():
        wgu_s[...] = wgu_ref[...].astype(BF16)
        wd_s[...] = wd_ref[...].astype(BF16)

    @pl.when(i < nused_ref[0])
    def _():
        h = _dot(x_ref[...].astype(BF16), wgu_s[...])
        hg = h[:, :D_FF]
        hu = h[:, D_FF:]
        act = (hg * _sigmoid(hg) * hu).astype(BF16)
        y_ref[...] = _dot(act, wd_s[...]).astype(y_ref.dtype)

    @pl.when(i >= nused_ref[0])
    def _():
        y_ref[...] = jnp.zeros_like(y_ref)


def _expert_ffn(x_pad, block_expert, n_used, w_gate_up, w_down, layer):
    grid_spec = pltpu.PrefetchScalarGridSpec(
        num_scalar_prefetch=2,
        grid=(MOE_BLOCKS,),
        in_specs=[pl.BlockSpec((MOE_ROWS, D_MODEL), lambda i, be, nu: (i, 0)),
                  pl.BlockSpec((None, None, D_MODEL, 2 * D_FF), lambda i, be, nu: (layer, be[i], 0, 0)),
                  pl.BlockSpec((None, None, D_FF, D_MODEL), lambda i, be, nu: (layer, be[i], 0, 0))],
        out_specs=pl.BlockSpec((MOE_ROWS, D_MODEL), lambda i, be, nu: (i, 0)),
        scratch_shapes=[pltpu.VMEM((D_MODEL, 2 * D_FF), BF16), pltpu.VMEM((D_FF, D_MODEL), BF16)],
    )
    return pl.pallas_call(
        _expert_kernel,
        grid_spec=grid_spec,
        out_shape=jax.ShapeDtypeStruct((MOE_PAD_ROWS, D_MODEL), F32),
        compiler_params=_params(("arbitrary",), 58 * 1024 * 1024),
        name="expert_ffn",
    )(block_expert, n_used, x_pad, w_gate_up, w_down)


def _combine_ln_kernel(res_ref, y0_ref, y1_ref, gate_ref, g_ref, b_ref, of_ref, ob_ref):
    gate = gate_ref[...]
    ffn = y0_ref[...] * gate[:, 0:1] + y1_ref[...] * gate[:, 1:2]
    out = _layer_norm(DEEPNORM_ALPHA * res_ref[...] + ffn, g_ref[...], b_ref[...])
    of_ref[...] = out
    ob_ref[...] = out.astype(BF16)


def _combine_ln(res, y0, y1, gate, g, b, tm=256):
    m, n = res.shape
    blk = pl.BlockSpec((tm, n), lambda i: (i, 0))
    row = pl.BlockSpec((1, n), lambda i: (0, 0))
    return pl.pallas_call(
        _combine_ln_kernel,
        grid=(m // tm,),
        in_specs=[blk, blk, blk, pl.BlockSpec((tm, TOP_K), lambda i: (i, 0)), row, row],
        out_specs=[blk, blk],
        out_shape=[jax.ShapeDtypeStruct((m, n), F32), jax.ShapeDtypeStruct((m, n), BF16)],
        compiler_params=_params(("parallel",)),
        name="combine_ln",
    )(res, y0, y1, gate, g.reshape(1, n), b.reshape(1, n))


def _dispatch_tables(idx):
    e_flat = idx.T.reshape(-1)
    onehot = (e_flat[:, None] == jnp.arange(N_EXPERTS, dtype=jnp.int32)[None, :]).astype(jnp.int32)
    csum = jnp.cumsum(onehot, axis=0)
    rank = jnp.take_along_axis(csum, e_flat[:, None], axis=1)[:, 0] - 1
    counts = csum[-1]
    padded = (counts + MOE_ROWS - 1) // MOE_ROWS * MOE_ROWS
    pad_end = jnp.cumsum(padded)
    dest = (pad_end - padded)[e_flat] + rank
    tok = jnp.arange(N_TOK * TOP_K, dtype=jnp.int32) // TOP_K
    src_tok = jnp.zeros((MOE_PAD_ROWS,), jnp.int32).at[dest].set(tok)
    starts = jnp.arange(MOE_BLOCKS, dtype=jnp.int32) * MOE_ROWS
    block_expert = jnp.minimum(jnp.sum((pad_end[None, :] <= starts[:, None]).astype(jnp.int32), axis=1),
                               N_EXPERTS - 1)
    n_used = (pad_end[-1:] // MOE_ROWS).astype(jnp.int32)
    return dest.reshape(N_TOK, TOP_K), src_tok, block_expert, n_used


def _moe_layer(xf, xb, router_w, w_gate_up, w_down, ln_g, ln_b, layer):
    wh, wl, bias = router_w
    idx, gate = _router(xf, wh, wl, bias)
    dest, src_tok, block_expert, n_used = _dispatch_tables(idx)
    x_pad = xf.at[src_tok].get(mode="promise_in_bounds")
    y_pad = _expert_ffn(x_pad, block_expert, n_used, w_gate_up, w_down, layer)
    y0 = y_pad.at[dest[:, 0]].get(mode="promise_in_bounds")
    y1 = y_pad.at[dest[:, 1]].get(mode="promise_in_bounds")
    return _combine_ln(xf, y0, y1, gate.T, ln_g, ln_b)


def _router_weights(w_router, router_bias):
    lane = np.arange(N_EXPERTS)
    perm = (lane % N_GROUPS) * EXPERTS_PER_GROUP + lane // N_GROUPS
    w = jnp.pad(w_router[:, perm], ((0, 0), (0, LANES - N_EXPERTS)))
    wh = w.astype(BF16)
    wl = (w - wh.astype(F32)).astype(BF16)
    return wh, wl, router_bias[perm].astype(F32).reshape(N_EXPERTS, 1)


def kernel(x, positions, w_in_hgrn, lb_param, g_norm_hgrn, w_out_hgrn, w_kv, b_kv, w_q_attn, b_q_attn, sinks, w_o_attn, b_o_attn, w_router, router_bias, w_gate_up, w_down, ln_mix_g, ln_mix_b, ln_ffn_g, ln_ffn_b):
    xf = x.reshape(N_TOK, D_MODEL)
    xb = xf.astype(BF16)
    pos = positions.reshape(N_TOK, 1).astype(F32)
    half = ROT_DIM // 2
    inv_freq = 1.0 / (ROPE_THETA ** (jnp.arange(half, dtype=F32) * 2.0 / ROT_DIM))
    lane = np.arange(LANES) % ATT_HEAD_DIM
    invf = jnp.where(lane < ROT_DIM, inv_freq[lane % half], 0.0).reshape(1, LANES).astype(F32)
    router_w = _router_weights(w_router, router_bias)
    zero_bias = jnp.zeros((D_MODEL,), F32)
    kv = None
    for layer in range(DEPTH):
        if layer < N_A_LAYERS:
            w_in = w_in_hgrn[layer]
            lbp = lb_param.astype(F32)
            part = lambda name, n: _hgrn_in_proj(
                xb, w_in[:, n * D_MODEL:(n + 1) * D_MODEL].astype(BF16), lbp, name, layer)
            (qs,) = part("q", 0)
            l2f, kg = part("f", 1)
            (v,) = part("v", 2)
            (gg,) = part("g", 3)
            o = _hgrn_scan(qs, kg, l2f, v, gg, g_norm_hgrn[layer])
            xf, xb = _matmul_ln(o, w_out_hgrn[layer].astype(BF16), zero_bias, xf,
                                ln_mix_g[layer], ln_mix_b[layer])
        else:
            if layer == N_A_LAYERS:
                kv = _proj_rotary(xb, w_kv.astype(BF16), b_kv, pos, invf,
                                  rot_chunks=ATT_KV_HEADS * ATT_HEAD_DIM // LANES, scale=1.0)
            j = layer - N_A_LAYERS
            q = _proj_rotary(xb, w_q_attn[j].astype(BF16), b_q_attn[j], pos, invf,
                             rot_chunks=D_MODEL // LANES, scale=ATT_HEAD_DIM ** -0.5)
            o = _attention(q, kv, sinks[j].astype(F32))
            xf, xb = _matmul_ln(o, w_o_attn[j].astype(BF16), b_o_attn[j], xf,
                                ln_mix_g[layer], ln_mix_b[layer])
        xf, xb = _moe_layer(xf, xb, router_w, w_gate_up, w_down, ln_ffn_g[layer], ln_ffn_b[layer], layer)
    return xf.reshape(BATCH, SEQ, D_MODEL)
```

```python
import functools
import math

import numpy as np
import jax
import jax.numpy as jnp
from jax import lax
from jax.experimental import pallas as pl
from jax.experimental.pallas import tpu as pltpu

D_MODEL = 2048
BATCH = 4
SEQ = 2048
DEPTH = 4
N_TOK = BATCH * SEQ

N_A_LAYERS = DEPTH // 2
N_B_LAYERS = DEPTH - N_A_LAYERS
HG_EXPAND = 128
HG_HEADS = D_MODEL // HG_EXPAND
HG_DK = HG_EXPAND
HG_DV = D_MODEL // HG_HEADS
LB_FLOOR = 1e-30
ATT_HEAD_DIM = 64
ATT_Q_HEADS = D_MODEL // ATT_HEAD_DIM
ATT_KV_HEADS = ATT_Q_HEADS // 8
ATT_GROUP = ATT_Q_HEADS // ATT_KV_HEADS
WINDOW = 128
ROT_DIM = ATT_HEAD_DIM // 4
ROPE_THETA = 500000.0
MASK_VALUE = -1e30
N_EXPERTS = 32
N_GROUPS = 8
EXPERTS_PER_GROUP = N_EXPERTS // N_GROUPS
TOP_K = 2
D_FF = (704 * D_MODEL) // 2048
ROUTE_MASK = -1e9
DEEPNORM_ALPHA = (2 * DEPTH) ** 0.25
LN_EPS = 1e-5
RMS_EPS = 1e-6

LANES = 128
SCAN_CHUNK = 64
SCAN_LEVELS = 6
SCAN_HEADS = 4
SCAN_ROWS = 16
MOE_ROWS = 256
MOE_BLOCKS = (N_TOK * TOP_K + N_EXPERTS * (MOE_ROWS - 1) + MOE_ROWS - 1) // MOE_ROWS
MOE_PAD_ROWS = MOE_BLOCKS * MOE_ROWS
VMEM_LIMIT = 56 * 1024 * 1024

F32 = jnp.float32
BF16 = jnp.bfloat16


def _dot(a, b):
    return jnp.dot(a, b, preferred_element_type=F32)


def _dot_nt(a, b):
    return lax.dot_general(a, b, (((1,), (1,)), ((), ())), preferred_element_type=F32)


def _dot_tn(a, b):
    return lax.dot_general(a, b, (((0,), (0,)), ((), ())), preferred_element_type=F32)


def _sigmoid(x):
    return 1.0 / (1.0 + jnp.exp(-x))


def _params(sem, limit=VMEM_LIMIT):
    return pltpu.CompilerParams(dimension_semantics=sem, vmem_limit_bytes=limit)


def _hgrn_in_kernel(a_ref, w_ref, lbp_ref, *o_refs, part, layer):
    acc = _dot(a_ref[...], w_ref[...])
    if part == "q":
        o_refs[0][...] = (acc * _sigmoid(acc) * (HG_DK ** -0.5)).astype(BF16)
    elif part == "v":
        o_refs[0][...] = acc.astype(BF16)
    elif part == "g":
        o_refs[0][...] = (acc * _sigmoid(acc)).astype(BF16)
    else:
        lbp = lbp_ref[...]
        e = jnp.exp(lbp - jnp.max(lbp, axis=0, keepdims=True))
        sm = e / jnp.sum(e, axis=0, keepdims=True)
        lb = jnp.zeros_like(sm[0:1])
        for r in range(1, layer + 1):
            lb = lb + sm[r:r + 1]
        ez = jnp.exp(-jnp.abs(acc))
        rz = 1.0 / (1.0 + ez)
        sig_pos = jnp.where(acc >= 0, rz, ez * rz)
        sig_neg = jnp.where(acc >= 0, ez * rz, rz)
        forget = jnp.maximum(lb, LB_FLOOR) + (1.0 - lb) * sig_pos
        o_refs[0][...] = jnp.log(forget) * (1.0 / math.log(2.0))
        o_refs[1][...] = ((1.0 - lb) * sig_neg).astype(BF16)


def _hgrn_in_proj(a, w, lb_param, part, layer, tm=512, tn=1024):
    m, k = a.shape
    n = w.shape[1]
    blk = pl.BlockSpec((tm, tn), lambda i, j: (i, j))
    if part == "f":
        out_specs = [blk, blk]
        out_shape = [jax.ShapeDtypeStruct((m, n), F32), jax.ShapeDtypeStruct((m, n), BF16)]
    else:
        out_specs = [blk]
        out_shape = [jax.ShapeDtypeStruct((m, n), BF16)]
    return pl.pallas_call(
        functools.partial(_hgrn_in_kernel, part=part, layer=layer),
        grid=(m // tm, n // tn),
        in_specs=[pl.BlockSpec((tm, k), lambda i, j: (i, 0)),
                  pl.BlockSpec((k, tn), lambda i, j: (0, j)),
                  pl.BlockSpec((N_A_LAYERS, tn), lambda i, j: (0, j))],
        out_specs=out_specs,
        out_shape=out_shape,
        compiler_params=_params(("parallel", "parallel")),
        name="hgrn_in_" + part,
    )(a, w, lb_param)


def _scan_constants():
    c = SCAN_CHUNK
    t = np.arange(c)[None, :]
    i = np.arange(c)[:, None]
    blocks = [(t <= i), (t > i)]
    masks = [np.eye(c, dtype=np.float32)]
    j = np.arange(c)[None, :]
    for lvl in range(1, SCAN_LEVELS + 1):
        s = 1 << lvl
        h = s // 2
        m = (i // s) * s + h - 1
        query_role = (i % s) >= h
        blocks.append(np.where(query_role, (t > m) & (t <= i), (t > i) & (t <= m)))
        masks.append((((i // s) == (j // s)) & query_role & ((j % s) < h)).astype(np.float32))
    a = np.concatenate(blocks, axis=0).astype(np.float32)
    return np.concatenate([a, a], axis=1), np.stack(masks)


def _query_row_blocks(lvl):
    size = 1 << lvl
    nblk = SCAN_CHUNK // SCAN_ROWS
    if size <= SCAN_ROWS:
        return list(range(nblk))
    return [r for r in range(nblk) if (r * SCAN_ROWS) % size >= size // 2]


def _hgrn_scan_kernel(qs_ref, k_ref, l2f_ref, v_ref, gg_ref, gn_ref, sums_ref, mask_ref, o_ref, st_ref, ex_ref):
    c = SCAN_CHUNK
    rb = SCAN_ROWS
    nblk = c // rb
    gn = gn_ref[...]
    st_ref[...] = jnp.zeros_like(st_ref)

    def decay_stage(r0, buf):
        for pair in range(SCAN_HEADS // 2):
            cols = slice(pair * 2 * LANES, (pair + 1) * 2 * LANES)
            l2f = l2f_ref[pl.ds(r0, c), cols]
            hi = l2f.astype(BF16)
            lo = (l2f - hi.astype(F32)).astype(BF16)
            ex_ref[buf, :, cols] = jnp.exp2(_dot(sums_ref[...], jnp.concatenate([hi, lo], axis=0)))

    every = list(range(nblk))

    def decays(buf, hd, block, rows):
        cols = slice(hd * LANES, (hd + 1) * LANES)
        return jnp.concatenate(
            [ex_ref[buf, block * c + r * rb:block * c + (r + 1) * rb, cols] for r in rows], axis=0)

    def intra_scores(r0, hd, buf):
        cols = slice(hd * LANES, (hd + 1) * LANES)
        qs = qs_ref[pl.ds(r0, c), cols]
        k = k_ref[pl.ds(r0, c), cols]
        qs32 = qs.astype(F32)
        k32 = k.astype(F32)
        diag = _dot_nt(qs, k)
        sc = [diag[r * rb:(r + 1) * rb] * mask_ref[0, r * rb:(r + 1) * rb, :] for r in every]
        for lvl in range(1, SCAN_LEVELS + 1):
            rows = _query_row_blocks(lvl)
            ql = jnp.concatenate([qs32[r * rb:(r + 1) * rb] for r in rows], axis=0) * decays(buf, hd, lvl + 1, rows)
            part = _dot_nt(ql.astype(BF16), (k32 * decays(buf, hd, lvl + 1, every)).astype(BF16))
            for n, r in enumerate(rows):
                sc[r] = sc[r] + part[n * rb:(n + 1) * rb] * mask_ref[lvl, r * rb:(r + 1) * rb, :]
        return jnp.concatenate(sc, axis=0).astype(BF16)

    def state_step(r0, hd, buf, scores):
        cols = slice(hd * LANES, (hd + 1) * LANES)
        qs32 = qs_ref[pl.ds(r0, c), cols].astype(F32)
        k32 = k_ref[pl.ds(r0, c), cols].astype(F32)
        v = v_ref[pl.ds(r0, c), cols]
        eb = decays(buf, hd, 0, every)
        ek = decays(buf, hd, 1, every)
        st = st_ref[hd]
        o = _dot_nt((qs32 * eb).astype(BF16), st.astype(BF16)) + _dot(scores, v)
        st_ref[hd] = st * eb[c - 1:c, :] + _dot_tn(v, (k32 * ek).astype(BF16))
        on = o * lax.rsqrt(jnp.mean(o * o, axis=-1, keepdims=True) + RMS_EPS)
        o_ref[pl.ds(r0, c), cols] = (on * gn * gg_ref[pl.ds(r0, c), cols].astype(F32)).astype(o_ref.dtype)

    decay_stage(0, 0)

    def chunk_pair(cp, carry):
        for half in range(2):
            r0 = pl.multiple_of(cp * (2 * c) + half * c, c)
            scores = [intra_scores(r0, hd, half) for hd in range(SCAN_HEADS)]
            decay_stage(pl.multiple_of(jnp.minimum(r0 + c, SEQ - c), c), 1 - half)
            for hd in range(SCAN_HEADS):
                state_step(r0, hd, half, scores[hd])
        return carry

    lax.fori_loop(0, SEQ // (2 * c), chunk_pair, 0)


def _hgrn_scan(qs, k, l2f, v, gg, g_norm):
    sums_np, masks_np = _scan_constants()
    sums_m = jnp.asarray(sums_np, BF16)
    masks = jnp.asarray(masks_np, F32)
    width = SCAN_HEADS * LANES
    blk = pl.BlockSpec((SEQ, width), lambda b, hg: (b, hg))
    return pl.pallas_call(
        _hgrn_scan_kernel,
        grid=(BATCH, HG_HEADS // SCAN_HEADS),
        in_specs=[blk, blk, blk, blk, blk,
                  pl.BlockSpec((1, LANES), lambda b, hg: (0, 0)),
                  pl.BlockSpec(sums_m.shape, lambda b, hg: (0, 0)),
                  pl.BlockSpec(masks.shape, lambda b, hg: (0, 0, 0))],
        out_specs=blk,
        out_shape=jax.ShapeDtypeStruct((N_TOK, D_MODEL), BF16),
        scratch_shapes=[pltpu.VMEM((SCAN_HEADS, HG_DV, HG_DK), F32),
                        pltpu.VMEM((2, (SCAN_LEVELS + 2) * SCAN_CHUNK, width), F32)],
        compiler_params=_params(("parallel", "parallel")),
        name="hgrn_scan",
    )(qs, k, l2f, v, gg, g_norm.reshape(1, HG_DV), sums_m, masks)


def _layer_norm(y, g, b):
    mu = jnp.mean(y, axis=-1, keepdims=True)
    d = y - mu
    var = jnp.mean(d * d, axis=-1, keepdims=True)
    return d * lax.rsqrt(var + LN_EPS) * g + b


def _matmul_ln_kernel(a_ref, w_ref, bias_ref, res_ref, g_ref, b_ref, of_ref, ob_ref):
    mix = _dot(a_ref[...], w_ref[...]) + bias_ref[...]
    out = _layer_norm(DEEPNORM_ALPHA * res_ref[...] + mix, g_ref[...], b_ref[...])
    of_ref[...] = out
    ob_ref[...] = out.astype(BF16)


def _matmul_ln(a, w, bias, res, g, b, tm=256):
    m, k = a.shape
    n = w.shape[1]
    row = lambda width: pl.BlockSpec((1, width), lambda i: (0, 0))
    return pl.pallas_call(
        _matmul_ln_kernel,
        grid=(m // tm,),
        in_specs=[pl.BlockSpec((tm, k), lambda i: (i, 0)),
                  pl.BlockSpec((k, n), lambda i: (0, 0)),
                  row(n),
                  pl.BlockSpec((tm, n), lambda i: (i, 0)),
                  row(n), row(n)],
        out_specs=[pl.BlockSpec((tm, n), lambda i: (i, 0)), pl.BlockSpec((tm, n), lambda i: (i, 0))],
        out_shape=[jax.ShapeDtypeStruct((m, n), F32), jax.ShapeDtypeStruct((m, n), BF16)],
        compiler_params=_params(("parallel",)),
        name="out_proj_ln",
    )(a, w, bias.reshape(1, n), res, g.reshape(1, n), b.reshape(1, n))


def _proj_rotary_kernel(a_ref, w_ref, bias_ref, pos_ref, invf_ref, o_ref, *, rot_chunks, scale):
    acc = _dot(a_ref[...], w_ref[...]) + bias_ref[...]
    ang = pos_ref[...] * invf_ref[...]
    lane = lax.broadcasted_iota(jnp.int32, (1, LANES), 1) % ATT_HEAD_DIM
    half = ROT_DIM // 2
    cos = jnp.cos(ang)
    sin = jnp.sin(ang)
    c_keep = jnp.where(lane < ROT_DIM, cos, 1.0)
    s_from_left = jnp.where((lane >= half) & (lane < ROT_DIM), sin, 0.0)
    s_from_right = jnp.where(lane < half, -sin, 0.0)
    for j in range(acc.shape[1] // LANES):
        t = acc[:, j * LANES:(j + 1) * LANES]
        if j < rot_chunks:
            t = (t * c_keep + pltpu.roll(t, half, 1) * s_from_left
                 + pltpu.roll(t, LANES - half, 1) * s_from_right)
        o_ref[:, j * LANES:(j + 1) * LANES] = (t * scale).astype(o_ref.dtype)


def _proj_rotary(a, w, bias, pos, invf, rot_chunks, scale, tm=512, tn=512):
    m, k = a.shape
    n = w.shape[1]
    return pl.pallas_call(
        functools.partial(_proj_rotary_kernel, rot_chunks=rot_chunks, scale=scale),
        grid=(m // tm, n // tn),
        in_specs=[pl.BlockSpec((tm, k), lambda i, j: (i, 0)),
                  pl.BlockSpec((k, tn), lambda i, j: (0, j)),
                  pl.BlockSpec((1, tn), lambda i, j: (0, j)),
                  pl.BlockSpec((tm, 1), lambda i, j: (i, 0)),
                  pl.BlockSpec((1, LANES), lambda i, j: (0, 0))],
        out_specs=pl.BlockSpec((tm, tn), lambda i, j: (i, j)),
        out_shape=jax.ShapeDtypeStruct((m, n), BF16),
        compiler_params=_params(("parallel", "parallel")),
        name="proj_rotary",
    )(a, w, bias.reshape(1, n), pos, invf)


def _attn_kernel(sink_ref, q_ref, kp_ref, kc_ref, vp_ref, vc_ref, o_ref):
    nb = pl.program_id(1)
    w = WINDOW
    hd = ATT_HEAD_DIM
    k = jnp.concatenate([kp_ref[...], kc_ref[...]], axis=0)
    v = jnp.concatenate([vp_ref[...], vc_ref[...]], axis=0)
    qi = lax.broadcasted_iota(jnp.int32, (w, 2 * w), 0) + w
    ki = lax.broadcasted_iota(jnp.int32, (w, 2 * w), 1)
    rel = qi - ki
    valid = (rel >= 0) & (rel < w) & ((ki >= w) | (nb > 0))
    for j in range(ATT_KV_HEADS):
        kj = k[:, j * hd:(j + 1) * hd]
        vj = v[:, j * hd:(j + 1) * hd]
        heads = [j * ATT_GROUP + gq for gq in range(ATT_GROUP)]
        qw = q_ref[:, j * ATT_GROUP * hd:(j + 1) * ATT_GROUP * hd]
        qj = jnp.concatenate([qw[:, gq * hd:(gq + 1) * hd] for gq in range(ATT_GROUP)], axis=0)
        s_all = _dot_nt(qj, kj)
        outs = []
        for gq, h in enumerate(heads):
            s = jnp.where(valid, s_all[gq * w:(gq + 1) * w], MASK_VALUE)
            sink = sink_ref[h]
            m = jnp.maximum(jnp.max(s, axis=-1, keepdims=True), sink)
            p = jnp.exp(s - m)
            denom = jnp.sum(p, axis=-1, keepdims=True) + jnp.exp(sink - m)
            outs.append(_dot(p.astype(BF16), vj) / denom)
        for gq in range(0, ATT_GROUP, 2):
            h = heads[gq]
            o_ref[:, h * hd:(h + 2) * hd] = jnp.concatenate(outs[gq:gq + 2], axis=1).astype(o_ref.dtype)


def _attention(q, kv, sinks):
    nblk = SEQ // WINDOW
    kvw = ATT_KV_HEADS * ATT_HEAD_DIM
    cur = lambda col: pl.BlockSpec((WINDOW, kvw), lambda b, n, s: (b * nblk + n, col))
    prev = lambda col: pl.BlockSpec((WINDOW, kvw), lambda b, n, s: (b * nblk + jnp.maximum(n - 1, 0), col))
    grid_spec = pltpu.PrefetchScalarGridSpec(
        num_scalar_prefetch=1,
        grid=(BATCH, nblk),
        in_specs=[pl.BlockSpec((WINDOW, D_MODEL), lambda b, n, s: (b * nblk + n, 0)),
                  prev(0), cur(0), prev(1), cur(1)],
        out_specs=pl.BlockSpec((WINDOW, D_MODEL), lambda b, n, s: (b * nblk + n, 0)),
    )
    return pl.pallas_call(
        _attn_kernel,
        grid_spec=grid_spec,
        out_shape=jax.ShapeDtypeStruct((N_TOK, D_MODEL), BF16),
        compiler_params=_params(("parallel", "parallel")),
        name="swa_attention",
    )(sinks, q, kv, kv, kv, kv)


def _router_kernel(x_ref, wh_ref, wl_ref, bias_ref, idx_ref, gate_ref):
    x = x_ref[...]
    xh = x.astype(BF16)
    xl = (x - xh.astype(F32)).astype(BF16)
    wh = wh_ref[...]
    logits = _dot(xh, wh) + _dot(xl, wh) + _dot(xh, wl_ref[...])
    lt = logits.T[0:N_EXPERTS]
    scores = _sigmoid(lt)
    biased = scores + bias_ref[...]
    g = N_GROUPS
    a, b, c, d = (biased[p * g:(p + 1) * g] for p in range(EXPERTS_PER_GROUP))
    group_score = jnp.maximum(jnp.maximum(a + b, c + d), jnp.maximum(a, b) + jnp.maximum(c, d))
    gi = lax.broadcasted_iota(jnp.int32, group_score.shape, 0)
    best = jnp.min(jnp.where(group_score == jnp.max(group_score, axis=0, keepdims=True), gi, g),
                   axis=0, keepdims=True)
    row = lax.broadcasted_iota(jnp.int32, biased.shape, 0)
    expert = (row % g) * EXPERTS_PER_GROUP + row // g
    masked = jnp.where((row % g) == best, biased, ROUTE_MASK)

    def take_top(vals):
        top = jnp.max(vals, axis=0, keepdims=True)
        return jnp.min(jnp.where(vals == top, expert, N_EXPERTS), axis=0, keepdims=True)

    first = take_top(masked)
    second = take_top(jnp.where(expert == first, -jnp.inf, masked))
    w1 = jnp.sum(jnp.where(expert == first, scores, 0.0), axis=0, keepdims=True)
    w2 = jnp.sum(jnp.where(expert == second, scores, 0.0), axis=0, keepdims=True)
    idx_ref[0:1, :] = first
    idx_ref[1:2, :] = second
    gate_ref[0:1, :] = w1 / (w1 + w2)
    gate_ref[1:2, :] = w2 / (w1 + w2)


def _router(x, wh, wl, bias, tm=512):
    m, k = x.shape
    return pl.pallas_call(
        _router_kernel,
        grid=(m // tm,),
        in_specs=[pl.BlockSpec((tm, k), lambda i: (i, 0)),
                  pl.BlockSpec((k, LANES), lambda i: (0, 0)),
                  pl.BlockSpec((k, LANES), lambda i: (0, 0)),
                  pl.BlockSpec((N_EXPERTS, 1), lambda i: (0, 0))],
        out_specs=[pl.BlockSpec((TOP_K, tm), lambda i: (0, i)), pl.BlockSpec((TOP_K, tm), lambda i: (0, i))],
        out_shape=[jax.ShapeDtypeStruct((TOP_K, m), jnp.int32), jax.ShapeDtypeStruct((TOP_K, m), F32)],
        compiler_params=_params(("parallel",)),
        name="router",
    )(x, wh, wl, bias)


def _expert_kernel(be_ref, nused_ref, next_ref, src_ref, x_hbm, wgu_hbm, wd_hbm, y_ref,
                   wgu_f, wd_f, wgu_b, wd_b, xbuf, wsem, xsem, *, layer):
    i = pl.program_id(0)
    n_used = nused_ref[0]
    e = be_ref[i]
    used = i < n_used
    first = (i == 0) | (e != be_ref[jnp.maximum(i - 1, 0)])
    slot = i % 2

    def weight_copies(expert):
        return (pltpu.make_async_copy(wgu_hbm.at[layer, expert], wgu_f, wsem.at[0]),
                pltpu.make_async_copy(wd_hbm.at[layer, expert], wd_f, wsem.at[1]))

    def row_copy(tok, to_slot, r):
        return pltpu.make_async_copy(x_hbm.at[pl.ds(tok, 1)], xbuf.at[to_slot, pl.ds(r, 1)], xsem.at[to_slot])

    def start_rows(block, to_slot):
        base = block * MOE_ROWS

        def body(r, carry):
            row_copy(src_ref[base + r], to_slot, r).start()
            return carry

        lax.fori_loop(0, MOE_ROWS, body, 0, unroll=8)

    def wait_rows(of_slot):
        def body(r, carry):
            row_copy(0, of_slot, r).wait()
            return carry

        lax.fori_loop(0, MOE_ROWS, body, 0, unroll=8)

    @pl.when(i == 0)
    def _():
        for cp in weight_copies(e):
            cp.start(priority=1)
        start_rows(0, 0)

    @pl.when(used & first)
    def _():
        for cp in weight_copies(e):
            cp.wait()
        wgu_b[...] = wgu_f[...].astype(BF16)
        wd_b[...] = wd_f[...].astype(BF16)
        nxt = next_ref[i]

        @pl.when(nxt >= 0)
        def _():
            for cp in weight_copies(nxt):
                cp.start(priority=1)

    @pl.when(used)
    def _():
        @pl.when(i + 1 < n_used)
        def _():
            start_rows(i + 1, 1 - slot)

        wait_rows(slot)
        h = _dot(xbuf[slot].astype(BF16), wgu_b[...])
        hg = h[:, :D_FF]
        hu = h[:, D_FF:]
        act = (hg * _sigmoid(hg) * hu).astype(BF16)
        y_ref[...] = _dot(act, wd_b[...])

    @pl.when(jnp.logical_not(used))
    def _():
        y_ref[...] = jnp.zeros_like(y_ref)


def _expert_ffn(x, src_tok, block_expert, n_used, next_expert, w_gate_up, w_down, layer):
    any_space = pl.BlockSpec(memory_space=pl.ANY)
    grid_spec = pltpu.PrefetchScalarGridSpec(
        num_scalar_prefetch=4,
        grid=(MOE_BLOCKS,),
        in_specs=[any_space, any_space, any_space],
        out_specs=pl.BlockSpec((MOE_ROWS, D_MODEL), lambda i, *_: (i, 0)),
        scratch_shapes=[pltpu.VMEM((D_MODEL, 2 * D_FF), F32), pltpu.VMEM((D_FF, D_MODEL), F32),
                        pltpu.VMEM((D_MODEL, 2 * D_FF), BF16), pltpu.VMEM((D_FF, D_MODEL), BF16),
                        pltpu.VMEM((2, MOE_ROWS, D_MODEL), F32),
                        pltpu.SemaphoreType.DMA((2,)), pltpu.SemaphoreType.DMA((2,))],
    )
    return pl.pallas_call(
        functools.partial(_expert_kernel, layer=layer),
        grid_spec=grid_spec,
        out_shape=jax.ShapeDtypeStruct((MOE_PAD_ROWS, D_MODEL), F32),
        compiler_params=_params(("arbitrary",)),
        name="expert_ffn",
    )(block_expert, n_used, next_expert, src_tok, x, w_gate_up, w_down)


def _combine_ln_kernel(dest_ref, res_ref, gate_ref, g_ref, b_ref, y_hbm, of_ref, ob_ref, ybuf, ysem):
    i = pl.program_id(0)
    slot = i % 2
    tm = res_ref.shape[0]

    def row_copy(row, to_slot, k, r):
        return pltpu.make_async_copy(y_hbm.at[pl.ds(row, 1)], ybuf.at[to_slot, k, pl.ds(r, 1)], ysem.at[to_slot])

    def start_rows(tile, to_slot):
        base = tile * (tm * TOP_K)

        def body(r, carry):
            for k in range(TOP_K):
                row_copy(dest_ref[base + r * TOP_K + k], to_slot, k, r).start()
            return carry

        lax.fori_loop(0, tm, body, 0, unroll=4)

    def wait_rows(of_slot):
        def body(r, carry):
            for k in range(TOP_K):
                row_copy(0, of_slot, k, r).wait()
            return carry

        lax.fori_loop(0, tm, body, 0, unroll=4)

    @pl.when(i == 0)
    def _():
        start_rows(0, 0)

    @pl.when(i + 1 < pl.num_programs(0))
    def _():
        start_rows(i + 1, 1 - slot)

    wait_rows(slot)
    gate = gate_ref[...]
    ffn = ybuf[slot, 0] * gate[:, 0:1] + ybuf[slot, 1] * gate[:, 1:2]
    out = _layer_norm(DEEPNORM_ALPHA * res_ref[...] + ffn, g_ref[...], b_ref[...])
    of_ref[...] = out
    ob_ref[...] = out.astype(BF16)


def _combine_ln(res, y_pad, dest, gate, g, b, tm=256):
    m, n = res.shape
    blk = pl.BlockSpec((tm, n), lambda i, d: (i, 0))
    row = pl.BlockSpec((1, n), lambda i, d: (0, 0))
    grid_spec = pltpu.PrefetchScalarGridSpec(
        num_scalar_prefetch=1,
        grid=(m // tm,),
        in_specs=[blk, pl.BlockSpec((tm, TOP_K), lambda i, d: (i, 0)), row, row,
                  pl.BlockSpec(memory_space=pl.ANY)],
        out_specs=[blk, blk],
        scratch_shapes=[pltpu.VMEM((2, TOP_K, tm, n), F32), pltpu.SemaphoreType.DMA((2,))],
    )
    return pl.pallas_call(
        _combine_ln_kernel,
        grid_spec=grid_spec,
        out_shape=[jax.ShapeDtypeStruct((m, n), F32), jax.ShapeDtypeStruct((m, n), BF16)],
        compiler_params=_params(("arbitrary",)),
        name="combine_ln",
    )(dest, res, gate, g.reshape(1, n), b.reshape(1, n), y_pad)


def _dispatch_tables(idx):
    e_flat = idx.T.reshape(-1)
    onehot = (e_flat[:, None] == jnp.arange(N_EXPERTS, dtype=jnp.int32)[None, :]).astype(jnp.int32)
    csum = jnp.cumsum(onehot, axis=0)
    rank = jnp.take_along_axis(csum, e_flat[:, None], axis=1)[:, 0] - 1
    counts = csum[-1]
    padded = (counts + MOE_ROWS - 1) // MOE_ROWS * MOE_ROWS
    pad_end = jnp.cumsum(padded)
    dest = (pad_end - padded)[e_flat] + rank
    tok = jnp.arange(N_TOK * TOP_K, dtype=jnp.int32) // TOP_K
    src_tok = jnp.zeros((MOE_PAD_ROWS,), jnp.int32).at[dest].set(tok)
    starts = jnp.arange(MOE_BLOCKS, dtype=jnp.int32) * MOE_ROWS
    block_expert = jnp.minimum(jnp.sum((pad_end[None, :] <= starts[:, None]).astype(jnp.int32), axis=1),
                               N_EXPERTS - 1)
    n_used = (pad_end[-1:] // MOE_ROWS).astype(jnp.int32)
    ids = jnp.arange(N_EXPERTS, dtype=jnp.int32)
    later = jnp.where((ids[None, :] > ids[:, None]) & (counts[None, :] > 0), ids[None, :], N_EXPERTS)
    next_nonempty = jnp.min(later, axis=1)
    next_expert = jnp.where(next_nonempty < N_EXPERTS, next_nonempty, -1)[block_expert].astype(jnp.int32)
    return dest, src_tok, block_expert, n_used, next_expert


def _moe_layer(xf, router_w, w_gate_up, w_down, ln_g, ln_b, layer):
    wh, wl, bias = router_w
    idx, gate = _router(xf, wh, wl, bias)
    dest, src_tok, block_expert, n_used, next_expert = _dispatch_tables(idx)
    y_pad = _expert_ffn(xf, src_tok, block_expert, n_used, next_expert, w_gate_up, w_down, layer)
    return _combine_ln(xf, y_pad, dest, gate.T, ln_g, ln_b)


def _router_weights(w_router, router_bias):
    lane = np.arange(N_EXPERTS)
    perm = (lane % N_GROUPS) * EXPERTS_PER_GROUP + lane // N_GROUPS
    w = jnp.pad(w_router[:, perm], ((0, 0), (0, LANES - N_EXPERTS)))
    wh = w.astype(BF16)
    wl = (w - wh.astype(F32)).astype(BF16)
    return wh, wl, router_bias[perm].astype(F32).reshape(N_EXPERTS, 1)


def kernel(x, positions, w_in_hgrn, lb_param, g_norm_hgrn, w_out_hgrn, w_kv, b_kv, w_q_attn, b_q_attn, sinks, w_o_attn, b_o_attn, w_router, router_bias, w_gate_up, w_down, ln_mix_g, ln_mix_b, ln_ffn_g, ln_ffn_b):
    xf = x.reshape(N_TOK, D_MODEL)
    xb = xf.astype(BF16)
    pos = positions.reshape(N_TOK, 1).astype(F32)
    half = ROT_DIM // 2
    inv_freq = 1.0 / (ROPE_THETA ** (jnp.arange(half, dtype=F32) * 2.0 / ROT_DIM))
    lane = np.arange(LANES) % ATT_HEAD_DIM
    invf = jnp.where(lane < ROT_DIM, inv_freq[lane % half], 0.0).reshape(1, LANES).astype(F32)
    router_w = _router_weights(w_router, router_bias)
    zero_bias = jnp.zeros((D_MODEL,), F32)
    kv = None
    for layer in range(DEPTH):
        if layer < N_A_LAYERS:
            w_in = w_in_hgrn[layer]
            lbp = lb_param.astype(F32)
            part = lambda name, n: _hgrn_in_proj(
                xb, w_in[:, n * D_MODEL:(n + 1) * D_MODEL].astype(BF16), lbp, name, layer)
            (qs,) = part("q", 0)
            l2f, kg = part("f", 1)
            (v,) = part("v", 2)
            (gg,) = part("g", 3)
            o = _hgrn_scan(qs, kg, l2f, v, gg, g_norm_hgrn[layer])
            xf, xb = _matmul_ln(o, w_out_hgrn[layer].astype(BF16), zero_bias, xf,
                                ln_mix_g[layer], ln_mix_b[layer])
        else:
            if layer == N_A_LAYERS:
                kv = _proj_rotary(xb, w_kv.astype(BF16), b_kv, pos, invf,
                                  rot_chunks=ATT_KV_HEADS * ATT_HEAD_DIM // LANES, scale=1.0)
            j = layer - N_A_LAYERS
            q = _proj_rotary(xb, w_q_attn[j].astype(BF16), b_q_attn[j], pos, invf,
                             rot_chunks=D_MODEL // LANES, scale=ATT_HEAD_DIM ** -0.5, tm=256, tn=D_MODEL)
            o = _attention(q, kv, sinks[j].astype(F32))
            xf, xb = _matmul_ln(o, w_o_attn[j].astype(BF16), b_o_attn[j], xf,
                                ln_mix_g[layer], ln_mix_b[layer])
        xf, xb = _moe_layer(xf, router_w, w_gate_up, w_down, ln_ffn_g[layer], ln_ffn_b[layer], layer)
    return xf.reshape(BATCH, SEQ, D_MODEL)
```

```python
import functools
import math

import numpy as np
import jax
import jax.numpy as jnp
from jax import lax
from jax.experimental import pallas as pl
from jax.experimental.pallas import tpu as pltpu

D_MODEL = 2048
BATCH = 4
SEQ = 2048
DEPTH = 4
N_TOK = BATCH * SEQ

N_A_LAYERS = DEPTH // 2
N_B_LAYERS = DEPTH - N_A_LAYERS
HG_EXPAND = 128
HG_HEADS = D_MODEL // HG_EXPAND
HG_DK = HG_EXPAND
HG_DV = D_MODEL // HG_HEADS
LB_FLOOR = 1e-30
ATT_HEAD_DIM = 64
ATT_Q_HEADS = D_MODEL // ATT_HEAD_DIM
ATT_KV_HEADS = ATT_Q_HEADS // 8
ATT_GROUP = ATT_Q_HEADS // ATT_KV_HEADS
WINDOW = 128
ROT_DIM = ATT_HEAD_DIM // 4
ROPE_THETA = 500000.0
MASK_VALUE = -1e30
N_EXPERTS = 32
N_GROUPS = 8
EXPERTS_PER_GROUP = N_EXPERTS // N_GROUPS
TOP_K = 2
D_FF = (704 * D_MODEL) // 2048
ROUTE_MASK = -1e9
DEEPNORM_ALPHA = (2 * DEPTH) ** 0.25
LN_EPS = 1e-5
RMS_EPS = 1e-6

LANES = 128
MXU_WIDTH = 256
SCAN_CHUNK = 64
SCAN_LEVELS = 6
SCAN_HEADS = 4
SCAN_ROWS = 16
MOE_ROWS = 256
MOE_BLOCKS = (N_TOK * TOP_K + N_EXPERTS * (MOE_ROWS - 1) + MOE_ROWS - 1) // MOE_ROWS
MOE_PAD_ROWS = MOE_BLOCKS * MOE_ROWS
WGU_STREAMS = 4
WD_STREAMS = 2
CAST_ROWS = 32
COMBINE_ROWS = 32
VMEM_LIMIT = 56 * 1024 * 1024

F32 = jnp.float32
BF16 = jnp.bfloat16


def _dot(a, b):
    return jnp.dot(a, b, preferred_element_type=F32)


def _dot_nt(a, b):
    return lax.dot_general(a, b, (((1,), (1,)), ((), ())), preferred_element_type=F32)


def _dot_tn(a, b):
    return lax.dot_general(a, b, (((0,), (0,)), ((), ())), preferred_element_type=F32)


def _sigmoid(x):
    return 1.0 / (1.0 + jnp.exp(-x))


def _params(sem, limit=VMEM_LIMIT):
    return pltpu.CompilerParams(dimension_semantics=sem, vmem_limit_bytes=limit)


def _hgrn_in_kernel(a_ref, w_ref, lbp_ref, *o_refs, part, layer):
    acc = _dot(a_ref[...], w_ref[...])
    if part == "q":
        o_refs[0][...] = (acc * _sigmoid(acc) * (HG_DK ** -0.5)).astype(BF16)
    elif part == "v":
        o_refs[0][...] = acc.astype(BF16)
    elif part == "g":
        o_refs[0][...] = (acc * _sigmoid(acc)).astype(BF16)
    else:
        lbp = lbp_ref[...]
        e = jnp.exp(lbp - jnp.max(lbp, axis=0, keepdims=True))
        sm = e / jnp.sum(e, axis=0, keepdims=True)
        lb = jnp.zeros_like(sm[0:1])
        for r in range(1, layer + 1):
            lb = lb + sm[r:r + 1]
        ez = jnp.exp(-jnp.abs(acc))
        rz = 1.0 / (1.0 + ez)
        sig_pos = jnp.where(acc >= 0, rz, ez * rz)
        sig_neg = jnp.where(acc >= 0, ez * rz, rz)
        forget = jnp.maximum(lb, LB_FLOOR) + (1.0 - lb) * sig_pos
        o_refs[0][...] = jnp.log(forget) * (1.0 / math.log(2.0))
        o_refs[1][...] = ((1.0 - lb) * sig_neg).astype(BF16)


def _hgrn_in_proj(a, w, lb_param, part, layer, tm=512, tn=1024):
    m, k = a.shape
    n = w.shape[1]
    blk = pl.BlockSpec((tm, tn), lambda i, j: (i, j))
    if part == "f":
        out_specs = [blk, blk]
        out_shape = [jax.ShapeDtypeStruct((m, n), F32), jax.ShapeDtypeStruct((m, n), BF16)]
    else:
        out_specs = [blk]
        out_shape = [jax.ShapeDtypeStruct((m, n), BF16)]
    return pl.pallas_call(
        functools.partial(_hgrn_in_kernel, part=part, layer=layer),
        grid=(m // tm, n // tn),
        in_specs=[pl.BlockSpec((tm, k), lambda i, j: (i, 0)),
                  pl.BlockSpec((k, tn), lambda i, j: (0, j)),
                  pl.BlockSpec((N_A_LAYERS, tn), lambda i, j: (0, j))],
        out_specs=out_specs,
        out_shape=out_shape,
        compiler_params=_params(("parallel", "parallel")),
        name="hgrn_in_" + part,
    )(a, w, lb_param)


def _scan_constants():
    c = SCAN_CHUNK
    t = np.arange(c)[None, :]
    i = np.arange(c)[:, None]
    blocks = [(t <= i), (t > i)]
    masks = [np.eye(c, dtype=np.float32)]
    j = np.arange(c)[None, :]
    for lvl in range(1, SCAN_LEVELS + 1):
        s = 1 << lvl
        h = s // 2
        m = (i // s) * s + h - 1
        query_role = (i % s) >= h
        blocks.append(np.where(query_role, (t > m) & (t <= i), (t > i) & (t <= m)))
        masks.append((((i // s) == (j // s)) & query_role & ((j % s) < h)).astype(np.float32))
    a = np.concatenate(blocks, axis=0).astype(np.float32)
    return np.concatenate([a, a], axis=1), np.stack(masks)


def _query_row_blocks(lvl):
    size = 1 << lvl
    nblk = SCAN_CHUNK // SCAN_ROWS
    if size <= SCAN_ROWS:
        return list(range(nblk))
    return [r for r in range(nblk) if (r * SCAN_ROWS) % size >= size // 2]


def _hgrn_scan_kernel(qs_ref, k_ref, l2f_ref, v_ref, gg_ref, gn_ref, sums_ref, mask_ref, o_ref, st_ref, ex_ref):
    c = SCAN_CHUNK
    rb = SCAN_ROWS
    nblk = c // rb
    gn = gn_ref[...]
    st_ref[...] = jnp.zeros_like(st_ref)

    def decay_stage(r0, buf):
        for pair in range(SCAN_HEADS // 2):
            cols = slice(pair * 2 * LANES, (pair + 1) * 2 * LANES)
            l2f = l2f_ref[pl.ds(r0, c), cols]
            hi = l2f.astype(BF16)
            lo = (l2f - hi.astype(F32)).astype(BF16)
            ex_ref[buf, :, cols] = jnp.exp2(_dot(sums_ref[...], jnp.concatenate([hi, lo], axis=0)))

    every = list(range(nblk))

    def decays(buf, hd, block, rows):
        cols = slice(hd * LANES, (hd + 1) * LANES)
        return jnp.concatenate(
            [ex_ref[buf, block * c + r * rb:block * c + (r + 1) * rb, cols] for r in rows], axis=0)

    def intra_scores(r0, hd, buf):
        cols = slice(hd * LANES, (hd + 1) * LANES)
        qs = qs_ref[pl.ds(r0, c), cols]
        k = k_ref[pl.ds(r0, c), cols]
        qs32 = qs.astype(F32)
        k32 = k.astype(F32)
        diag = _dot_nt(qs, k)
        sc = [diag[r * rb:(r + 1) * rb] * mask_ref[0, r * rb:(r + 1) * rb, :] for r in every]
        for lvl in range(1, SCAN_LEVELS + 1):
            rows = _query_row_blocks(lvl)
            ql = jnp.concatenate([qs32[r * rb:(r + 1) * rb] for r in rows], axis=0) * decays(buf, hd, lvl + 1, rows)
            part = _dot_nt(ql.astype(BF16), (k32 * decays(buf, hd, lvl + 1, every)).astype(BF16))
            for n, r in enumerate(rows):
                sc[r] = sc[r] + part[n * rb:(n + 1) * rb] * mask_ref[lvl, r * rb:(r + 1) * rb, :]
        return jnp.concatenate(sc, axis=0).astype(BF16)

    def state_step(r0, hd, buf, scores):
        cols = slice(hd * LANES, (hd + 1) * LANES)
        qs32 = qs_ref[pl.ds(r0, c), cols].astype(F32)
        k32 = k_ref[pl.ds(r0, c), cols].astype(F32)
        v = v_ref[pl.ds(r0, c), cols]
        eb = decays(buf, hd, 0, every)
        ek = decays(buf, hd, 1, every)
        st = st_ref[hd]
        o = _dot_nt((qs32 * eb).astype(BF16), st.astype(BF16)) + _dot(scores, v)
        st_ref[hd] = st * eb[c - 1:c, :] + _dot_tn(v, (k32 * ek).astype(BF16))
        on = o * lax.rsqrt(jnp.mean(o * o, axis=-1, keepdims=True) + RMS_EPS)
        o_ref[pl.ds(r0, c), cols] = (on * gn * gg_ref[pl.ds(r0, c), cols].astype(F32)).astype(o_ref.dtype)

    decay_stage(0, 0)

    def chunk_pair(cp, carry):
        for half in range(2):
            r0 = pl.multiple_of(cp * (2 * c) + half * c, c)
            scores = [intra_scores(r0, hd, half) for hd in range(SCAN_HEADS)]
            decay_stage(pl.multiple_of(jnp.minimum(r0 + c, SEQ - c), c), 1 - half)
            for hd in range(SCAN_HEADS):
                state_step(r0, hd, half, scores[hd])
        return carry

    lax.fori_loop(0, SEQ // (2 * c), chunk_pair, 0)


def _hgrn_scan(qs, k, l2f, v, gg, g_norm):
    sums_np, masks_np = _scan_constants()
    sums_m = jnp.asarray(sums_np, BF16)
    masks = jnp.asarray(masks_np, F32)
    width = SCAN_HEADS * LANES
    blk = pl.BlockSpec((SEQ, width), lambda b, hg: (b, hg))
    return pl.pallas_call(
        _hgrn_scan_kernel,
        grid=(BATCH, HG_HEADS // SCAN_HEADS),
        in_specs=[blk, blk, blk, blk, blk,
                  pl.BlockSpec((1, LANES), lambda b, hg: (0, 0)),
                  pl.BlockSpec(sums_m.shape, lambda b, hg: (0, 0)),
                  pl.BlockSpec(masks.shape, lambda b, hg: (0, 0, 0))],
        out_specs=blk,
        out_shape=jax.ShapeDtypeStruct((N_TOK, D_MODEL), BF16),
        scratch_shapes=[pltpu.VMEM((SCAN_HEADS, HG_DV, HG_DK), F32),
                        pltpu.VMEM((2, (SCAN_LEVELS + 2) * SCAN_CHUNK, width), F32)],
        compiler_params=_params(("parallel", "parallel")),
        name="hgrn_scan",
    )(qs, k, l2f, v, gg, g_norm.reshape(1, HG_DV), sums_m, masks)


def _layer_norm(y, g, b):
    mu = jnp.mean(y, axis=-1, keepdims=True)
    d = y - mu
    var = jnp.mean(d * d, axis=-1, keepdims=True)
    return d * lax.rsqrt(var + LN_EPS) * g + b


def _matmul_ln_kernel(a_ref, w_ref, bias_ref, res_ref, g_ref, b_ref, of_ref, ob_ref):
    mix = _dot(a_ref[...], w_ref[...]) + bias_ref[...]
    out = _layer_norm(DEEPNORM_ALPHA * res_ref[...] + mix, g_ref[...], b_ref[...])
    of_ref[...] = out
    ob_ref[...] = out.astype(BF16)


def _matmul_ln(a, w, bias, res, g, b, tm=256):
    m, k = a.shape
    n = w.shape[1]
    row = lambda width: pl.BlockSpec((1, width), lambda i: (0, 0))
    return pl.pallas_call(
        _matmul_ln_kernel,
        grid=(m // tm,),
        in_specs=[pl.BlockSpec((tm, k), lambda i: (i, 0)),
                  pl.BlockSpec((k, n), lambda i: (0, 0)),
                  row(n),
                  pl.BlockSpec((tm, n), lambda i: (i, 0)),
                  row(n), row(n)],
        out_specs=[pl.BlockSpec((tm, n), lambda i: (i, 0)), pl.BlockSpec((tm, n), lambda i: (i, 0))],
        out_shape=[jax.ShapeDtypeStruct((m, n), F32), jax.ShapeDtypeStruct((m, n), BF16)],
        compiler_params=_params(("parallel",)),
        name="out_proj_ln",
    )(a, w, bias.reshape(1, n), res, g.reshape(1, n), b.reshape(1, n))


def _proj_rotary_kernel(a_ref, w_ref, bias_ref, pos_ref, invf_ref, o_ref, *, rot_chunks, scale):
    acc = _dot(a_ref[...], w_ref[...]) + bias_ref[...]
    ang = pos_ref[...] * invf_ref[...]
    lane = lax.broadcasted_iota(jnp.int32, (1, LANES), 1) % ATT_HEAD_DIM
    half = ROT_DIM // 2
    cos = jnp.cos(ang)
    sin = jnp.sin(ang)
    c_keep = jnp.where(lane < ROT_DIM, cos, 1.0)
    s_from_left = jnp.where((lane >= half) & (lane < ROT_DIM), sin, 0.0)
    s_from_right = jnp.where(lane < half, -sin, 0.0)
    for j in range(acc.shape[1] // LANES):
        t = acc[:, j * LANES:(j + 1) * LANES]
        if j < rot_chunks:
            t = (t * c_keep + pltpu.roll(t, half, 1) * s_from_left
                 + pltpu.roll(t, LANES - half, 1) * s_from_right)
        o_ref[:, j * LANES:(j + 1) * LANES] = (t * scale).astype(o_ref.dtype)


def _proj_rotary(a, w, bias, pos, invf, rot_chunks, scale, tm=512, tn=512):
    m, k = a.shape
    n = w.shape[1]
    return pl.pallas_call(
        functools.partial(_proj_rotary_kernel, rot_chunks=rot_chunks, scale=scale),
        grid=(m // tm, n // tn),
        in_specs=[pl.BlockSpec((tm, k), lambda i, j: (i, 0)),
                  pl.BlockSpec((k, tn), lambda i, j: (0, j)),
                  pl.BlockSpec((1, tn), lambda i, j: (0, j)),
                  pl.BlockSpec((tm, 1), lambda i, j: (i, 0)),
                  pl.BlockSpec((1, LANES), lambda i, j: (0, 0))],
        out_specs=pl.BlockSpec((tm, tn), lambda i, j: (i, j)),
        out_shape=jax.ShapeDtypeStruct((m, n), BF16),
        compiler_params=_params(("parallel", "parallel")),
        name="proj_rotary",
    )(a, w, bias.reshape(1, n), pos, invf)


def _attn_kernel(sink_ref, q_ref, kp_ref, kc_ref, vp_ref, vc_ref, o_ref):
    nb = pl.program_id(1)
    w = WINDOW
    hd = ATT_HEAD_DIM
    k = jnp.concatenate([kp_ref[...], kc_ref[...]], axis=0)
    v = jnp.concatenate([vp_ref[...], vc_ref[...]], axis=0)
    qi = lax.broadcasted_iota(jnp.int32, (w, 2 * w), 0) + w
    ki = lax.broadcasted_iota(jnp.int32, (w, 2 * w), 1)
    rel = qi - ki
    valid = (rel >= 0) & (rel < w) & ((ki >= w) | (nb > 0))
    for j in range(ATT_KV_HEADS):
        kj = k[:, j * hd:(j + 1) * hd]
        vj = v[:, j * hd:(j + 1) * hd]
        heads = [j * ATT_GROUP + gq for gq in range(ATT_GROUP)]
        qw = q_ref[:, j * ATT_GROUP * hd:(j + 1) * ATT_GROUP * hd]
        qj = jnp.concatenate([qw[:, gq * hd:(gq + 1) * hd] for gq in range(ATT_GROUP)], axis=0)
        s_all = _dot_nt(qj, kj)
        outs = []
        for gq, h in enumerate(heads):
            s = jnp.where(valid, s_all[gq * w:(gq + 1) * w], MASK_VALUE)
            sink = sink_ref[h]
            m = jnp.maximum(jnp.max(s, axis=-1, keepdims=True), sink)
            p = jnp.exp(s - m)
            denom = jnp.sum(p, axis=-1, keepdims=True) + jnp.exp(sink - m)
            outs.append(_dot(p.astype(BF16), vj) / denom)
        for gq in range(0, ATT_GROUP, 2):
            h = heads[gq]
            o_ref[:, h * hd:(h + 2) * hd] = jnp.concatenate(outs[gq:gq + 2], axis=1).astype(o_ref.dtype)


def _attention(q, kv, sinks):
    nblk = SEQ // WINDOW
    kvw = ATT_KV_HEADS * ATT_HEAD_DIM
    cur = lambda col: pl.BlockSpec((WINDOW, kvw), lambda b, n, s: (b * nblk + n, col))
    prev = lambda col: pl.BlockSpec((WINDOW, kvw), lambda b, n, s: (b * nblk + jnp.maximum(n - 1, 0), col))
    grid_spec = pltpu.PrefetchScalarGridSpec(
        num_scalar_prefetch=1,
        grid=(BATCH, nblk),
        in_specs=[pl.BlockSpec((WINDOW, D_MODEL), lambda b, n, s: (b * nblk + n, 0)),
                  prev(0), cur(0), prev(1), cur(1)],
        out_specs=pl.BlockSpec((WINDOW, D_MODEL), lambda b, n, s: (b * nblk + n, 0)),
    )
    return pl.pallas_call(
        _attn_kernel,
        grid_spec=grid_spec,
        out_shape=jax.ShapeDtypeStruct((N_TOK, D_MODEL), BF16),
        compiler_params=_params(("parallel", "parallel")),
        name="swa_attention",
    )(sinks, q, kv, kv, kv, kv)


def _router_kernel(x_ref, wh_ref, wl_ref, bias_ref, idx_ref, gate_ref):
    x = x_ref[...]
    xh = x.astype(BF16)
    xl = (x - xh.astype(F32)).astype(BF16)
    wh = wh_ref[...]
    logits = _dot(xh, wh) + _dot(xl, wh) + _dot(xh, wl_ref[...])
    lt = logits.T[0:N_EXPERTS]
    scores = _sigmoid(lt)
    biased = scores + bias_ref[...]
    g = N_GROUPS
    a, b, c, d = (biased[p * g:(p + 1) * g] for p in range(EXPERTS_PER_GROUP))
    group_score = jnp.maximum(jnp.maximum(a + b, c + d), jnp.maximum(a, b) + jnp.maximum(c, d))
    gi = lax.broadcasted_iota(jnp.int32, group_score.shape, 0)
    best = jnp.min(jnp.where(group_score == jnp.max(group_score, axis=0, keepdims=True), gi, g),
                   axis=0, keepdims=True)
    row = lax.broadcasted_iota(jnp.int32, biased.shape, 0)
    expert = (row % g) * EXPERTS_PER_GROUP + row // g
    masked = jnp.where((row % g) == best, biased, ROUTE_MASK)

    def take_top(vals):
        top = jnp.max(vals, axis=0, keepdims=True)
        return jnp.min(jnp.where(vals == top, expert, N_EXPERTS), axis=0, keepdims=True)

    first = take_top(masked)
    second = take_top(jnp.where(expert == first, -jnp.inf, masked))
    w1 = jnp.sum(jnp.where(expert == first, scores, 0.0), axis=0, keepdims=True)
    w2 = jnp.sum(jnp.where(expert == second, scores, 0.0), axis=0, keepdims=True)
    idx_ref[0:1, :] = first
    idx_ref[1:2, :] = second
    gate_ref[0:1, :] = w1 / (w1 + w2)
    gate_ref[1:2, :] = w2 / (w1 + w2)


def _router(x, wh, wl, bias, tm=512):
    m, k = x.shape
    return pl.pallas_call(
        _router_kernel,
        grid=(m // tm,),
        in_specs=[pl.BlockSpec((tm, k), lambda i: (i, 0)),
                  pl.BlockSpec((k, LANES), lambda i: (0, 0)),
                  pl.BlockSpec((k, LANES), lambda i: (0, 0)),
                  pl.BlockSpec((N_EXPERTS, 1), lambda i: (0, 0))],
        out_specs=[pl.BlockSpec((TOP_K, tm), lambda i: (0, i)), pl.BlockSpec((TOP_K, tm), lambda i: (0, i))],
        out_shape=[jax.ShapeDtypeStruct((TOP_K, m), jnp.int32), jax.ShapeDtypeStruct((TOP_K, m), F32)],
        compiler_params=_params(("parallel",)),
        name="router",
    )(x, wh, wl, bias)


def _expert_kernel(be_ref, nused_ref, next_ref, src_ref, x_hbm, wgu_hbm, wd_hbm, y_ref,
                   wgu_f, wd_f, wgu_b, wd_b, xbuf, wsem, xsem, *, layer):
    i = pl.program_id(0)
    n_used = nused_ref[0]
    e = be_ref[i]
    used = i < n_used
    first = (i == 0) | (e != be_ref[jnp.maximum(i - 1, 0)])
    slot = i % 2

    def weight_copies(expert):
        copies = []
        for n, (hbm, buf, parts) in enumerate(((wgu_hbm, wgu_f, WGU_STREAMS), (wd_hbm, wd_f, WD_STREAMS))):
            rows = buf.shape[0] // parts
            for p in range(parts):
                part = pl.ds(p * rows, rows)
                copies.append(pltpu.make_async_copy(hbm.at[layer, expert, part], buf.at[part],
                                                    wsem.at[n * WGU_STREAMS + p]))
        return copies

    def row_copy(tok, to_slot, r):
        return pltpu.make_async_copy(x_hbm.at[pl.ds(tok, 1)], xbuf.at[to_slot, pl.ds(r, 1)], xsem.at[to_slot])

    def start_rows(block, to_slot):
        base = block * MOE_ROWS
        for r in range(MOE_ROWS):
            row_copy(src_ref[base + r], to_slot, r).start()

    def wait_rows(of_slot):
        for r in range(MOE_ROWS):
            row_copy(0, of_slot, r).wait()

    @pl.when(i == 0)
    def _():
        for cp in weight_copies(e):
            cp.start(priority=1)
        start_rows(0, 0)

    @pl.when(used & first)
    def _():
        for cp in weight_copies(e):
            cp.wait()
        for f32_ref, bf16_ref in ((wgu_f, wgu_b), (wd_f, wd_b)):
            def cast_rows(c, carry, f32_ref=f32_ref, bf16_ref=bf16_ref):
                rows = pl.ds(pl.multiple_of(c * CAST_ROWS, CAST_ROWS), CAST_ROWS)
                bf16_ref[rows, :] = f32_ref[rows, :].astype(BF16)
                return carry

            lax.fori_loop(0, f32_ref.shape[0] // CAST_ROWS, cast_rows, 0)
        nxt = next_ref[i]

        @pl.when(nxt >= 0)
        def _():
            for cp in weight_copies(nxt):
                cp.start(priority=1)

    @pl.when(used)
    def _():
        wait_rows(slot)
        base = jnp.minimum(i + 1, n_used - 1) * MOE_ROWS
        x = xbuf[slot].astype(BF16)
        width = 2 * MXU_WIDTH
        n_chunks = -(-2 * D_FF // width)
        per_chunk = -(-MOE_ROWS // n_chunks)
        parts = []
        for c in range(n_chunks):
            parts.append(_dot(x, wgu_b[:, c * width:min((c + 1) * width, 2 * D_FF)]))
            for r in range(c * per_chunk, min((c + 1) * per_chunk, MOE_ROWS)):
                row_copy(src_ref[base + r], 1 - slot, r).start()
        h = jnp.concatenate(parts, axis=1)
        hg = h[:, :D_FF]
        hu = h[:, D_FF:]
        act = (hg * _sigmoid(hg) * hu).astype(BF16)
        y_ref[...] = _dot(act, wd_b[...])

    @pl.when(used & (i == n_used - 1))
    def _():
        wait_rows(1 - slot)

    @pl.when(jnp.logical_not(used))
    def _():
        y_ref[...] = jnp.zeros_like(y_ref)


def _expert_ffn(x, src_tok, block_expert, n_used, next_expert, w_gate_up, w_down, layer):
    any_space = pl.BlockSpec(memory_space=pl.ANY)
    grid_spec = pltpu.PrefetchScalarGridSpec(
        num_scalar_prefetch=4,
        grid=(MOE_BLOCKS,),
        in_specs=[any_space, any_space, any_space],
        out_specs=pl.BlockSpec((MOE_ROWS, D_MODEL), lambda i, *_: (i, 0)),
        scratch_shapes=[pltpu.VMEM((D_MODEL, 2 * D_FF), F32), pltpu.VMEM((D_FF, D_MODEL), F32),
                        pltpu.VMEM((D_MODEL, 2 * D_FF), BF16), pltpu.VMEM((D_FF, D_MODEL), BF16),
                        pltpu.VMEM((2, MOE_ROWS, D_MODEL), F32),
                        pltpu.SemaphoreType.DMA((WGU_STREAMS + WD_STREAMS,)), pltpu.SemaphoreType.DMA((2,))],
    )
    return pl.pallas_call(
        functools.partial(_expert_kernel, layer=layer),
        grid_spec=grid_spec,
        out_shape=jax.ShapeDtypeStruct((MOE_PAD_ROWS, D_MODEL), F32),
        compiler_params=_params(("arbitrary",)),
        name="expert_ffn",
    )(block_expert, n_used, next_expert, src_tok, x, w_gate_up, w_down)


def _combine_ln_kernel(dest_ref, res_ref, gate_ref, g_ref, b_ref, y_hbm, of_ref, ob_ref, ybuf, ysem):
    i = pl.program_id(0)
    slot = i % 2
    tm = res_ref.shape[0]

    def row_copy(row, to_slot, k, r):
        return pltpu.make_async_copy(y_hbm.at[pl.ds(row, 1)], ybuf.at[to_slot, k, pl.ds(r, 1)], ysem.at[to_slot])

    def start_rows(base, to_slot, rows):
        for r in rows:
            for k in range(TOP_K):
                row_copy(dest_ref[base + r * TOP_K + k], to_slot, k, r).start()

    def wait_rows(of_slot):
        for r in range(tm):
            for k in range(TOP_K):
                row_copy(0, of_slot, k, r).wait()

    @pl.when(i == 0)
    def _():
        start_rows(0, 0, range(tm))

    wait_rows(slot)
    last = pl.num_programs(0) - 1
    base = jnp.minimum(i + 1, last) * (tm * TOP_K)
    g = g_ref[...]
    b = b_ref[...]
    for c in range(tm // COMBINE_ROWS):
        rows = slice(c * COMBINE_ROWS, (c + 1) * COMBINE_ROWS)
        gate = gate_ref[rows, :]
        ffn = ybuf[slot, 0, rows, :] * gate[:, 0:1] + ybuf[slot, 1, rows, :] * gate[:, 1:2]
        out = _layer_norm(DEEPNORM_ALPHA * res_ref[rows, :] + ffn, g, b)
        of_ref[rows, :] = out
        ob_ref[rows, :] = out.astype(BF16)
        start_rows(base, 1 - slot, range(c * COMBINE_ROWS, (c + 1) * COMBINE_ROWS))

    @pl.when(i == last)
    def _():
        wait_rows(1 - slot)


def _combine_ln(res, y_pad, dest, gate, g, b, tm=256):
    m, n = res.shape
    blk = pl.BlockSpec((tm, n), lambda i, d: (i, 0))
    row = pl.BlockSpec((1, n), lambda i, d: (0, 0))
    grid_spec = pltpu.PrefetchScalarGridSpec(
        num_scalar_prefetch=1,
        grid=(m // tm,),
        in_specs=[blk, pl.BlockSpec((tm, TOP_K), lambda i, d: (i, 0)), row, row,
                  pl.BlockSpec(memory_space=pl.ANY)],
        out_specs=[blk, blk],
        scratch_shapes=[pltpu.VMEM((2, TOP_K, tm, n), F32), pltpu.SemaphoreType.DMA((2,))],
    )
    return pl.pallas_call(
        _combine_ln_kernel,
        grid_spec=grid_spec,
        out_shape=[jax.ShapeDtypeStruct((m, n), F32), jax.ShapeDtypeStruct((m, n), BF16)],
        compiler_params=_params(("arbitrary",)),
        name="combine_ln",
    )(dest, res, gate, g.reshape(1, n), b.reshape(1, n), y_pad)


def _dispatch_tables(idx):
    e_flat = idx.T.reshape(-1)
    onehot = (e_flat[:, None] == jnp.arange(N_EXPERTS, dtype=jnp.int32)[None, :]).astype(jnp.int32)
    csum = jnp.cumsum(onehot, axis=0)
    rank = jnp.take_along_axis(csum, e_flat[:, None], axis=1)[:, 0] - 1
    counts = csum[-1]
    padded = (counts + MOE_ROWS - 1) // MOE_ROWS * MOE_ROWS
    pad_end = jnp.cumsum(padded)
    dest = (pad_end - padded)[e_flat] + rank
    tok = jnp.arange(N_TOK * TOP_K, dtype=jnp.int32) // TOP_K
    src_tok = jnp.zeros((MOE_PAD_ROWS,), jnp.int32).at[dest].set(tok)
    starts = jnp.arange(MOE_BLOCKS, dtype=jnp.int32) * MOE_ROWS
    block_expert = jnp.minimum(jnp.sum((pad_end[None, :] <= starts[:, None]).astype(jnp.int32), axis=1),
                               N_EXPERTS - 1)
    n_used = (pad_end[-1:] // MOE_ROWS).astype(jnp.int32)
    ids = jnp.arange(N_EXPERTS, dtype=jnp.int32)
    later = jnp.where((ids[None, :] > ids[:, None]) & (counts[None, :] > 0), ids[None, :], N_EXPERTS)
    next_nonempty = jnp.min(later, axis=1)
    next_expert = jnp.where(next_nonempty < N_EXPERTS, next_nonempty, -1)[block_expert].astype(jnp.int32)
    return dest, src_tok, block_expert, n_used, next_expert


def _moe_layer(xf, router_w, w_gate_up, w_down, ln_g, ln_b, layer):
    wh, wl, bias = router_w
    idx, gate = _router(xf, wh, wl, bias)
    dest, src_tok, block_expert, n_used, next_expert = _dispatch_tables(idx)
    y_pad = _expert_ffn(xf, src_tok, block_expert, n_used, next_expert, w_gate_up, w_down, layer)
    return _combine_ln(xf, y_pad, dest, gate.T, ln_g, ln_b)


def _router_weights(w_router, router_bias):
    lane = np.arange(N_EXPERTS)
    perm = (lane % N_GROUPS) * EXPERTS_PER_GROUP + lane // N_GROUPS
    w = jnp.pad(w_router[:, perm], ((0, 0), (0, LANES - N_EXPERTS)))
    wh = w.astype(BF16)
    wl = (w - wh.astype(F32)).astype(BF16)
    return wh, wl, router_bias[perm].astype(F32).reshape(N_EXPERTS, 1)


def kernel(x, positions, w_in_hgrn, lb_param, g_norm_hgrn, w_out_hgrn, w_kv, b_kv, w_q_attn, b_q_attn, sinks, w_o_attn, b_o_attn, w_router, router_bias, w_gate_up, w_down, ln_mix_g, ln_mix_b, ln_ffn_g, ln_ffn_b):
    xf = x.reshape(N_TOK, D_MODEL)
    xb = xf.astype(BF16)
    pos = positions.reshape(N_TOK, 1).astype(F32)
    half = ROT_DIM // 2
    inv_freq = 1.0 / (ROPE_THETA ** (jnp.arange(half, dtype=F32) * 2.0 / ROT_DIM))
    lane = np.arange(LANES) % ATT_HEAD_DIM
    invf = jnp.where(lane < ROT_DIM, inv_freq[lane % half], 0.0).reshape(1, LANES).astype(F32)
    router_w = _router_weights(w_router, router_bias)
    zero_bias = jnp.zeros((D_MODEL,), F32)
    kv = None
    for layer in range(DEPTH):
        if layer < N_A_LAYERS:
            w_in = w_in_hgrn[layer]
            lbp = lb_param.astype(F32)
            part = lambda name, n: _hgrn_in_proj(
                xb, w_in[:, n * D_MODEL:(n + 1) * D_MODEL].astype(BF16), lbp, name, layer)
            (qs,) = part("q", 0)
            l2f, kg = part("f", 1)
            (v,) = part("v", 2)
            (gg,) = part("g", 3)
            o = _hgrn_scan(qs, kg, l2f, v, gg, g_norm_hgrn[layer])
            xf, xb = _matmul_ln(o, w_out_hgrn[layer].astype(BF16), zero_bias, xf,
                                ln_mix_g[layer], ln_mix_b[layer])
        else:
            if layer == N_A_LAYERS:
                kv = _proj_rotary(xb, w_kv.astype(BF16), b_kv, pos, invf,
                                  rot_chunks=ATT_KV_HEADS * ATT_HEAD_DIM // LANES, scale=1.0)
            j = layer - N_A_LAYERS
            q = _proj_rotary(xb, w_q_attn[j].astype(BF16), b_q_attn[j], pos, invf,
                             rot_chunks=D_MODEL // LANES, scale=ATT_HEAD_DIM ** -0.5, tm=256, tn=D_MODEL)
            o = _attention(q, kv, sinks[j].astype(F32))
            xf, xb = _matmul_ln(o, w_o_attn[j].astype(BF16), b_o_attn[j], xf,
                                ln_mix_g[layer], ln_mix_b[layer])
        xf, xb = _moe_layer(xf, router_w, w_gate_up, w_down, ln_ffn_g[layer], ln_ffn_b[layer], layer)
    return xf.reshape(BATCH, SEQ, D_MODEL)
```

```python
import functools
import math

import numpy as np
import jax
import jax.numpy as jnp
from jax import lax
from jax.experimental import pallas as pl
from jax.experimental.pallas import tpu as pltpu

D_MODEL = 2048
BATCH = 4
SEQ = 2048
DEPTH = 4
N_TOK = BATCH * SEQ

N_A_LAYERS = DEPTH // 2
N_B_LAYERS = DEPTH - N_A_LAYERS
HG_EXPAND = 128
HG_HEADS = D_MODEL // HG_EXPAND
HG_DK = HG_EXPAND
HG_DV = D_MODEL // HG_HEADS
LB_FLOOR = 1e-30
ATT_HEAD_DIM = 64
ATT_Q_HEADS = D_MODEL // ATT_HEAD_DIM
ATT_KV_HEADS = ATT_Q_HEADS // 8
ATT_GROUP = ATT_Q_HEADS // ATT_KV_HEADS
WINDOW = 128
ROT_DIM = ATT_HEAD_DIM // 4
ROPE_THETA = 500000.0
MASK_VALUE = -1e30
N_EXPERTS = 32
N_GROUPS = 8
EXPERTS_PER_GROUP = N_EXPERTS // N_GROUPS
TOP_K = 2
D_FF = (704 * D_MODEL) // 2048
ROUTE_MASK = -1e9
DEEPNORM_ALPHA = (2 * DEPTH) ** 0.25
LN_EPS = 1e-5
RMS_EPS = 1e-6

LANES = 128
MXU_WIDTH = 256
SCAN_CHUNK = 64
SCAN_LEVELS = 6
SCAN_HEADS = 4
SCAN_ROWS = 16
MOE_ROWS = 256
MOE_BLOCKS = (N_TOK * TOP_K + N_EXPERTS * (MOE_ROWS - 1) + MOE_ROWS - 1) // MOE_ROWS
MOE_PAD_ROWS = MOE_BLOCKS * MOE_ROWS
WEIGHT_DMA_THREAD = 0
ROW_DMA_THREAD = 1
CAST_ROWS = 32
COMBINE_ROWS = 32
VMEM_LIMIT = 56 * 1024 * 1024

F32 = jnp.float32
BF16 = jnp.bfloat16


def _dot(a, b):
    return jnp.dot(a, b, preferred_element_type=F32)


def _dot_nt(a, b):
    return lax.dot_general(a, b, (((1,), (1,)), ((), ())), preferred_element_type=F32)


def _dot_tn(a, b):
    return lax.dot_general(a, b, (((0,), (0,)), ((), ())), preferred_element_type=F32)


def _sigmoid(x):
    return 1.0 / (1.0 + jnp.exp(-x))


def _params(sem, limit=VMEM_LIMIT):
    return pltpu.CompilerParams(dimension_semantics=sem, vmem_limit_bytes=limit)


def _hgrn_in_kernel(a_ref, w_ref, lbp_ref, *o_refs, part, layer):
    acc = _dot(a_ref[...], w_ref[...])
    if part == "q":
        o_refs[0][...] = (acc * _sigmoid(acc) * (HG_DK ** -0.5)).astype(BF16)
    elif part == "v":
        o_refs[0][...] = acc.astype(BF16)
    elif part == "g":
        o_refs[0][...] = (acc * _sigmoid(acc)).astype(BF16)
    else:
        lbp = lbp_ref[...]
        e = jnp.exp(lbp - jnp.max(lbp, axis=0, keepdims=True))
        sm = e / jnp.sum(e, axis=0, keepdims=True)
        lb = jnp.zeros_like(sm[0:1])
        for r in range(1, layer + 1):
            lb = lb + sm[r:r + 1]
        ez = jnp.exp(-jnp.abs(acc))
        rz = 1.0 / (1.0 + ez)
        sig_pos = jnp.where(acc >= 0, rz, ez * rz)
        sig_neg = jnp.where(acc >= 0, ez * rz, rz)
        forget = jnp.maximum(lb, LB_FLOOR) + (1.0 - lb) * sig_pos
        o_refs[0][...] = jnp.log(forget) * (1.0 / math.log(2.0))
        o_refs[1][...] = ((1.0 - lb) * sig_neg).astype(BF16)


def _hgrn_in_proj(a, w, lb_param, part, layer, tm=512, tn=1024):
    m, k = a.shape
    n = w.shape[1]
    blk = pl.BlockSpec((tm, tn), lambda i, j: (i, j))
    if part == "f":
        out_specs = [blk, blk]
        out_shape = [jax.ShapeDtypeStruct((m, n), F32), jax.ShapeDtypeStruct((m, n), BF16)]
    else:
        out_specs = [blk]
        out_shape = [jax.ShapeDtypeStruct((m, n), BF16)]
    return pl.pallas_call(
        functools.partial(_hgrn_in_kernel, part=part, layer=layer),
        grid=(m // tm, n // tn),
        in_specs=[pl.BlockSpec((tm, k), lambda i, j: (i, 0)),
                  pl.BlockSpec((k, tn), lambda i, j: (0, j)),
                  pl.BlockSpec((N_A_LAYERS, tn), lambda i, j: (0, j))],
        out_specs=out_specs,
        out_shape=out_shape,
        compiler_params=_params(("parallel", "parallel")),
        name="hgrn_in_" + part,
    )(a, w, lb_param)


def _scan_constants():
    c = SCAN_CHUNK
    t = np.arange(c)[None, :]
    i = np.arange(c)[:, None]
    blocks = [(t <= i), (t > i)]
    masks = [np.eye(c, dtype=np.float32)]
    j = np.arange(c)[None, :]
    for lvl in range(1, SCAN_LEVELS + 1):
        s = 1 << lvl
        h = s // 2
        m = (i // s) * s + h - 1
        query_role = (i % s) >= h
        blocks.append(np.where(query_role, (t > m) & (t <= i), (t > i) & (t <= m)))
        masks.append((((i // s) == (j // s)) & query_role & ((j % s) < h)).astype(np.float32))
    a = np.concatenate(blocks, axis=0).astype(np.float32)
    return np.concatenate([a, a], axis=1), np.stack(masks)


def _query_row_blocks(lvl):
    size = 1 << lvl
    nblk = SCAN_CHUNK // SCAN_ROWS
    if size <= SCAN_ROWS:
        return list(range(nblk))
    return [r for r in range(nblk) if (r * SCAN_ROWS) % size >= size // 2]


def _hgrn_scan_kernel(qs_ref, k_ref, l2f_ref, v_ref, gg_ref, gn_ref, sums_ref, mask_ref, o_ref, st_ref, ex_ref):
    c = SCAN_CHUNK
    rb = SCAN_ROWS
    nblk = c // rb
    gn = gn_ref[...]
    st_ref[...] = jnp.zeros_like(st_ref)

    def decay_stage(r0, buf):
        for pair in range(SCAN_HEADS // 2):
            cols = slice(pair * 2 * LANES, (pair + 1) * 2 * LANES)
            l2f = l2f_ref[pl.ds(r0, c), cols]
            hi = l2f.astype(BF16)
            lo = (l2f - hi.astype(F32)).astype(BF16)
            ex_ref[buf, :, cols] = jnp.exp2(_dot(sums_ref[...], jnp.concatenate([hi, lo], axis=0)))

    every = list(range(nblk))

    def decays(buf, hd, block, rows):
        cols = slice(hd * LANES, (hd + 1) * LANES)
        return jnp.concatenate(
            [ex_ref[buf, block * c + r * rb:block * c + (r + 1) * rb, cols] for r in rows], axis=0)

    def intra_scores(r0, hd, buf):
        cols = slice(hd * LANES, (hd + 1) * LANES)
        qs = qs_ref[pl.ds(r0, c), cols]
        k = k_ref[pl.ds(r0, c), cols]
        qs32 = qs.astype(F32)
        k32 = k.astype(F32)
        diag = _dot_nt(qs, k)
        sc = [diag[r * rb:(r + 1) * rb] * mask_ref[0, r * rb:(r + 1) * rb, :] for r in every]
        for lvl in range(1, SCAN_LEVELS + 1):
            rows = _query_row_blocks(lvl)
            ql = jnp.concatenate([qs32[r * rb:(r + 1) * rb] for r in rows], axis=0) * decays(buf, hd, lvl + 1, rows)
            part = _dot_nt(ql.astype(BF16), (k32 * decays(buf, hd, lvl + 1, every)).astype(BF16))
            for n, r in enumerate(rows):
                sc[r] = sc[r] + part[n * rb:(n + 1) * rb] * mask_ref[lvl, r * rb:(r + 1) * rb, :]
        return jnp.concatenate(sc, axis=0).astype(BF16)

    def state_step(r0, hd, buf, scores):
        cols = slice(hd * LANES, (hd + 1) * LANES)
        qs32 = qs_ref[pl.ds(r0, c), cols].astype(F32)
        k32 = k_ref[pl.ds(r0, c), cols].astype(F32)
        v = v_ref[pl.ds(r0, c), cols]
        eb = decays(buf, hd, 0, every)
        ek = decays(buf, hd, 1, every)
        st = st_ref[hd]
        o = _dot_nt((qs32 * eb).astype(BF16), st.astype(BF16)) + _dot(scores, v)
        st_ref[hd] = st * eb[c - 1:c, :] + _dot_tn(v, (k32 * ek).astype(BF16))
        on = o * lax.rsqrt(jnp.mean(o * o, axis=-1, keepdims=True) + RMS_EPS)
        o_ref[pl.ds(r0, c), cols] = (on * gn * gg_ref[pl.ds(r0, c), cols].astype(F32)).astype(o_ref.dtype)

    decay_stage(0, 0)

    def chunk_pair(cp, carry):
        for half in range(2):
            r0 = pl.multiple_of(cp * (2 * c) + half * c, c)
            scores = [intra_scores(r0, hd, half) for hd in range(SCAN_HEADS)]
            decay_stage(pl.multiple_of(jnp.minimum(r0 + c, SEQ - c), c), 1 - half)
            for hd in range(SCAN_HEADS):
                state_step(r0, hd, half, scores[hd])
        return carry

    lax.fori_loop(0, SEQ // (2 * c), chunk_pair, 0)


def _hgrn_scan(qs, k, l2f, v, gg, g_norm):
    sums_np, masks_np = _scan_constants()
    sums_m = jnp.asarray(sums_np, BF16)
    masks = jnp.asarray(masks_np, F32)
    width = SCAN_HEADS * LANES
    blk = pl.BlockSpec((SEQ, width), lambda b, hg: (b, hg))
    return pl.pallas_call(
        _hgrn_scan_kernel,
        grid=(BATCH, HG_HEADS // SCAN_HEADS),
        in_specs=[blk, blk, blk, blk, blk,
                  pl.BlockSpec((1, LANES), lambda b, hg: (0, 0)),
                  pl.BlockSpec(sums_m.shape, lambda b, hg: (0, 0)),
                  pl.BlockSpec(masks.shape, lambda b, hg: (0, 0, 0))],
        out_specs=blk,
        out_shape=jax.ShapeDtypeStruct((N_TOK, D_MODEL), BF16),
        scratch_shapes=[pltpu.VMEM((SCAN_HEADS, HG_DV, HG_DK), F32),
                        pltpu.VMEM((2, (SCAN_LEVELS + 2) * SCAN_CHUNK, width), F32)],
        compiler_params=_params(("parallel", "parallel")),
        name="hgrn_scan",
    )(qs, k, l2f, v, gg, g_norm.reshape(1, HG_DV), sums_m, masks)


def _layer_norm(y, g, b):
    mu = jnp.mean(y, axis=-1, keepdims=True)
    d = y - mu
    var = jnp.mean(d * d, axis=-1, keepdims=True)
    return d * lax.rsqrt(var + LN_EPS) * g + b


def _matmul_ln_kernel(a_ref, w_ref, bias_ref, res_ref, g_ref, b_ref, of_ref, ob_ref):
    mix = _dot(a_ref[...], w_ref[...]) + bias_ref[...]
    out = _layer_norm(DEEPNORM_ALPHA * res_ref[...] + mix, g_ref[...], b_ref[...])
    of_ref[...] = out
    ob_ref[...] = out.astype(BF16)


def _matmul_ln(a, w, bias, res, g, b, tm=256):
    m, k = a.shape
    n = w.shape[1]
    row = lambda width: pl.BlockSpec((1, width), lambda i: (0, 0))
    return pl.pallas_call(
        _matmul_ln_kernel,
        grid=(m // tm,),
        in_specs=[pl.BlockSpec((tm, k), lambda i: (i, 0)),
                  pl.BlockSpec((k, n), lambda i: (0, 0)),
                  row(n),
                  pl.BlockSpec((tm, n), lambda i: (i, 0)),
                  row(n), row(n)],
        out_specs=[pl.BlockSpec((tm, n), lambda i: (i, 0)), pl.BlockSpec((tm, n), lambda i: (i, 0))],
        out_shape=[jax.ShapeDtypeStruct((m, n), F32), jax.ShapeDtypeStruct((m, n), BF16)],
        compiler_params=_params(("parallel",)),
        name="out_proj_ln",
    )(a, w, bias.reshape(1, n), res, g.reshape(1, n), b.reshape(1, n))


def _proj_rotary_kernel(a_ref, w_ref, bias_ref, pos_ref, invf_ref, o_ref, *, rot_chunks, scale):
    acc = _dot(a_ref[...], w_ref[...]) + bias_ref[...]
    ang = pos_ref[...] * invf_ref[...]
    lane = lax.broadcasted_iota(jnp.int32, (1, LANES), 1) % ATT_HEAD_DIM
    half = ROT_DIM // 2
    cos = jnp.cos(ang)
    sin = jnp.sin(ang)
    c_keep = jnp.where(lane < ROT_DIM, cos, 1.0)
    s_from_left = jnp.where((lane >= half) & (lane < ROT_DIM), sin, 0.0)
    s_from_right = jnp.where(lane < half, -sin, 0.0)
    for j in range(acc.shape[1] // LANES):
        t = acc[:, j * LANES:(j + 1) * LANES]
        if j < rot_chunks:
            t = (t * c_keep + pltpu.roll(t, half, 1) * s_from_left
                 + pltpu.roll(t, LANES - half, 1) * s_from_right)
        o_ref[:, j * LANES:(j + 1) * LANES] = (t * scale).astype(o_ref.dtype)


def _proj_rotary(a, w, bias, pos, invf, rot_chunks, scale, tm=512, tn=512):
    m, k = a.shape
    n = w.shape[1]
    return pl.pallas_call(
        functools.partial(_proj_rotary_kernel, rot_chunks=rot_chunks, scale=scale),
        grid=(m // tm, n // tn),
        in_specs=[pl.BlockSpec((tm, k), lambda i, j: (i, 0)),
                  pl.BlockSpec((k, tn), lambda i, j: (0, j)),
                  pl.BlockSpec((1, tn), lambda i, j: (0, j)),
                  pl.BlockSpec((tm, 1), lambda i, j: (i, 0)),
                  pl.BlockSpec((1, LANES), lambda i, j: (0, 0))],
        out_specs=pl.BlockSpec((tm, tn), lambda i, j: (i, j)),
        out_shape=jax.ShapeDtypeStruct((m, n), BF16),
        compiler_params=_params(("parallel", "parallel")),
        name="proj_rotary",
    )(a, w, bias.reshape(1, n), pos, invf)


def _attn_kernel(sink_ref, q_ref, kp_ref, kc_ref, vp_ref, vc_ref, o_ref):
    nb = pl.program_id(1)
    w = WINDOW
    hd = ATT_HEAD_DIM
    k = jnp.concatenate([kp_ref[...], kc_ref[...]], axis=0)
    v = jnp.concatenate([vp_ref[...], vc_ref[...]], axis=0)
    qi = lax.broadcasted_iota(jnp.int32, (w, 2 * w), 0) + w
    ki = lax.broadcasted_iota(jnp.int32, (w, 2 * w), 1)
    rel = qi - ki
    valid = (rel >= 0) & (rel < w) & ((ki >= w) | (nb > 0))
    for j in range(ATT_KV_HEADS):
        kj = k[:, j * hd:(j + 1) * hd]
        vj = v[:, j * hd:(j + 1) * hd]
        heads = [j * ATT_GROUP + gq for gq in range(ATT_GROUP)]
        qw = q_ref[:, j * ATT_GROUP * hd:(j + 1) * ATT_GROUP * hd]
        qj = jnp.concatenate([qw[:, gq * hd:(gq + 1) * hd] for gq in range(ATT_GROUP)], axis=0)
        s_all = _dot_nt(qj, kj)
        outs = []
        for gq, h in enumerate(heads):
            s = jnp.where(valid, s_all[gq * w:(gq + 1) * w], MASK_VALUE)
            sink = sink_ref[h]
            m = jnp.maximum(jnp.max(s, axis=-1, keepdims=True), sink)
            p = jnp.exp(s - m)
            denom = jnp.sum(p, axis=-1, keepdims=True) + jnp.exp(sink - m)
            outs.append(_dot(p.astype(BF16), vj) / denom)
        for gq in range(0, ATT_GROUP, 2):
            h = heads[gq]
            o_ref[:, h * hd:(h + 2) * hd] = jnp.concatenate(outs[gq:gq + 2], axis=1).astype(o_ref.dtype)


def _attention(q, kv, sinks):
    nblk = SEQ // WINDOW
    kvw = ATT_KV_HEADS * ATT_HEAD_DIM
    cur = lambda col: pl.BlockSpec((WINDOW, kvw), lambda b, n, s: (b * nblk + n, col))
    prev = lambda col: pl.BlockSpec((WINDOW, kvw), lambda b, n, s: (b * nblk + jnp.maximum(n - 1, 0), col))
    grid_spec = pltpu.PrefetchScalarGridSpec(
        num_scalar_prefetch=1,
        grid=(BATCH, nblk),
        in_specs=[pl.BlockSpec((WINDOW, D_MODEL), lambda b, n, s: (b * nblk + n, 0)),
                  prev(0), cur(0), prev(1), cur(1)],
        out_specs=pl.BlockSpec((WINDOW, D_MODEL), lambda b, n, s: (b * nblk + n, 0)),
    )
    return pl.pallas_call(
        _attn_kernel,
        grid_spec=grid_spec,
        out_shape=jax.ShapeDtypeStruct((N_TOK, D_MODEL), BF16),
        compiler_params=_params(("parallel", "parallel")),
        name="swa_attention",
    )(sinks, q, kv, kv, kv, kv)


def _router_kernel(x_ref, wh_ref, wl_ref, bias_ref, idx_ref, gate_ref):
    x = x_ref[...]
    xh = x.astype(BF16)
    xl = (x - xh.astype(F32)).astype(BF16)
    wh = wh_ref[...]
    logits = _dot(xh, wh) + _dot(xl, wh) + _dot(xh, wl_ref[...])
    lt = logits.T[0:N_EXPERTS]
    scores = _sigmoid(lt)
    biased = scores + bias_ref[...]
    g = N_GROUPS
    a, b, c, d = (biased[p * g:(p + 1) * g] for p in range(EXPERTS_PER_GROUP))
    group_score = jnp.maximum(jnp.maximum(a + b, c + d), jnp.maximum(a, b) + jnp.maximum(c, d))
    gi = lax.broadcasted_iota(jnp.int32, group_score.shape, 0)
    best = jnp.min(jnp.where(group_score == jnp.max(group_score, axis=0, keepdims=True), gi, g),
                   axis=0, keepdims=True)
    row = lax.broadcasted_iota(jnp.int32, biased.shape, 0)
    expert = (row % g) * EXPERTS_PER_GROUP + row // g
    masked = jnp.where((row % g) == best, biased, ROUTE_MASK)

    def take_top(vals):
        top = jnp.max(vals, axis=0, keepdims=True)
        return jnp.min(jnp.where(vals == top, expert, N_EXPERTS), axis=0, keepdims=True)

    first = take_top(masked)
    second = take_top(jnp.where(expert == first, -jnp.inf, masked))
    w1 = jnp.sum(jnp.where(expert == first, scores, 0.0), axis=0, keepdims=True)
    w2 = jnp.sum(jnp.where(expert == second, scores, 0.0), axis=0, keepdims=True)
    idx_ref[0:1, :] = first
    idx_ref[1:2, :] = second
    gate_ref[0:1, :] = w1 / (w1 + w2)
    gate_ref[1:2, :] = w2 / (w1 + w2)


def _router(x, wh, wl, bias, tm=512):
    m, k = x.shape
    return pl.pallas_call(
        _router_kernel,
        grid=(m // tm,),
        in_specs=[pl.BlockSpec((tm, k), lambda i: (i, 0)),
                  pl.BlockSpec((k, LANES), lambda i: (0, 0)),
                  pl.BlockSpec((k, LANES), lambda i: (0, 0)),
                  pl.BlockSpec((N_EXPERTS, 1), lambda i: (0, 0))],
        out_specs=[pl.BlockSpec((TOP_K, tm), lambda i: (0, i)), pl.BlockSpec((TOP_K, tm), lambda i: (0, i))],
        out_shape=[jax.ShapeDtypeStruct((TOP_K, m), jnp.int32), jax.ShapeDtypeStruct((TOP_K, m), F32)],
        compiler_params=_params(("parallel",)),
        name="router",
    )(x, wh, wl, bias)


def _expert_kernel(be_ref, nused_ref, next_ref, src_ref, x_hbm, wgu_hbm, wd_hbm, y_ref,
                   wgu_f, wd_f, wgu_b, wd_b, xbuf, wsem, xsem, *, layer):
    i = pl.program_id(0)
    n_used = nused_ref[0]
    e = be_ref[i]
    used = i < n_used
    first = (i == 0) | (e != be_ref[jnp.maximum(i - 1, 0)])
    slot = i % 2

    def weight_copies(expert):
        return (pltpu.make_async_copy(wgu_hbm.at[layer, expert], wgu_f, wsem.at[0]),
                pltpu.make_async_copy(wd_hbm.at[layer, expert], wd_f, wsem.at[1]))

    def row_copy(tok, to_slot, r):
        return pltpu.make_async_copy(x_hbm.at[pl.ds(tok, 1)], xbuf.at[to_slot, pl.ds(r, 1)], xsem.at[to_slot])

    def start_rows(block, to_slot):
        base = block * MOE_ROWS
        for r in range(MOE_ROWS):
            row_copy(src_ref[base + r], to_slot, r).start(priority=ROW_DMA_THREAD)

    def wait_rows(of_slot):
        for r in range(MOE_ROWS):
            row_copy(0, of_slot, r).wait()

    @pl.when(i == 0)
    def _():
        for cp in weight_copies(e):
            cp.start(priority=WEIGHT_DMA_THREAD)
        start_rows(0, 0)

    @pl.when(used & first)
    def _():
        for cp in weight_copies(e):
            cp.wait()
        for f32_ref, bf16_ref in ((wgu_f, wgu_b), (wd_f, wd_b)):
            def cast_rows(c, carry, f32_ref=f32_ref, bf16_ref=bf16_ref):
                rows = pl.ds(pl.multiple_of(c * CAST_ROWS, CAST_ROWS), CAST_ROWS)
                bf16_ref[rows, :] = f32_ref[rows, :].astype(BF16)
                return carry

            lax.fori_loop(0, f32_ref.shape[0] // CAST_ROWS, cast_rows, 0)
        nxt = next_ref[i]

        @pl.when(nxt >= 0)
        def _():
            for cp in weight_copies(nxt):
                cp.start(priority=WEIGHT_DMA_THREAD)

    @pl.when(used)
    def _():
        wait_rows(slot)
        base = jnp.minimum(i + 1, n_used - 1) * MOE_ROWS
        x = xbuf[slot].astype(BF16)
        width = 2 * MXU_WIDTH
        n_chunks = -(-2 * D_FF // width)
        per_chunk = -(-MOE_ROWS // n_chunks)
        parts = []
        for c in range(n_chunks):
            parts.append(_dot(x, wgu_b[:, c * width:min((c + 1) * width, 2 * D_FF)]))
            for r in range(c * per_chunk, min((c + 1) * per_chunk, MOE_ROWS)):
                row_copy(src_ref[base + r], 1 - slot, r).start(priority=ROW_DMA_THREAD)
        h = jnp.concatenate(parts, axis=1)
        hg = h[:, :D_FF]
        hu = h[:, D_FF:]
        act = (hg * _sigmoid(hg) * hu).astype(BF16)
        y_ref[...] = _dot(act, wd_b[...])

    @pl.when(used & (i == n_used - 1))
    def _():
        wait_rows(1 - slot)

    @pl.when(jnp.logical_not(used))
    def _():
        y_ref[...] = jnp.zeros_like(y_ref)


def _expert_ffn(x, src_tok, block_expert, n_used, next_expert, w_gate_up, w_down, layer):
    any_space = pl.BlockSpec(memory_space=pl.ANY)
    grid_spec = pltpu.PrefetchScalarGridSpec(
        num_scalar_prefetch=4,
        grid=(MOE_BLOCKS,),
        in_specs=[any_space, any_space, any_space],
        out_specs=pl.BlockSpec((MOE_ROWS, D_MODEL), lambda i, *_: (i, 0)),
        scratch_shapes=[pltpu.VMEM((D_MODEL, 2 * D_FF), F32), pltpu.VMEM((D_FF, D_MODEL), F32),
                        pltpu.VMEM((D_MODEL, 2 * D_FF), BF16), pltpu.VMEM((D_FF, D_MODEL), BF16),
                        pltpu.VMEM((2, MOE_ROWS, D_MODEL), F32),
                        pltpu.SemaphoreType.DMA((2,)), pltpu.SemaphoreType.DMA((2,))],
    )
    return pl.pallas_call(
        functools.partial(_expert_kernel, layer=layer),
        grid_spec=grid_spec,
        out_shape=jax.ShapeDtypeStruct((MOE_PAD_ROWS, D_MODEL), F32),
        compiler_params=_params(("arbitrary",)),
        name="expert_ffn",
    )(block_expert, n_used, next_expert, src_tok, x, w_gate_up, w_down)


def _combine_ln_kernel(dest_ref, res_ref, gate_ref, g_ref, b_ref, y_hbm, of_ref, ob_ref, ybuf, ysem):
    i = pl.program_id(0)
    slot = i % 2
    tm = res_ref.shape[0]

    def row_copy(row, to_slot, k, r):
        return pltpu.make_async_copy(y_hbm.at[pl.ds(row, 1)], ybuf.at[to_slot, k, pl.ds(r, 1)], ysem.at[to_slot])

    def start_rows(base, to_slot, rows):
        for r in rows:
            for k in range(TOP_K):
                row_copy(dest_ref[base + r * TOP_K + k], to_slot, k, r).start()

    def wait_rows(of_slot):
        for r in range(tm):
            for k in range(TOP_K):
                row_copy(0, of_slot, k, r).wait()

    @pl.when(i == 0)
    def _():
        start_rows(0, 0, range(tm))

    wait_rows(slot)
    last = pl.num_programs(0) - 1
    base = jnp.minimum(i + 1, last) * (tm * TOP_K)
    g = g_ref[...]
    b = b_ref[...]
    for c in range(tm // COMBINE_ROWS):
        rows = slice(c * COMBINE_ROWS, (c + 1) * COMBINE_ROWS)
        gate = gate_ref[rows, :]
        ffn = ybuf[slot, 0, rows, :] * gate[:, 0:1] + ybuf[slot, 1, rows, :] * gate[:, 1:2]
        out = _layer_norm(DEEPNORM_ALPHA * res_ref[rows, :] + ffn, g, b)
        of_ref[rows, :] = out
        ob_ref[rows, :] = out.astype(BF16)
        start_rows(base, 1 - slot, range(c * COMBINE_ROWS, (c + 1) * COMBINE_ROWS))

    @pl.when(i == last)
    def _():
        wait_rows(1 - slot)


def _combine_ln(res, y_pad, dest, gate, g, b, tm=256):
    m, n = res.shape
    blk = pl.BlockSpec((tm, n), lambda i, d: (i, 0))
    row = pl.BlockSpec((1, n), lambda i, d: (0, 0))
    grid_spec = pltpu.PrefetchScalarGridSpec(
        num_scalar_prefetch=1,
        grid=(m // tm,),
        in_specs=[blk, pl.BlockSpec((tm, TOP_K), lambda i, d: (i, 0)), row, row,
                  pl.BlockSpec(memory_space=pl.ANY)],
        out_specs=[blk, blk],
        scratch_shapes=[pltpu.VMEM((2, TOP_K, tm, n), F32), pltpu.SemaphoreType.DMA((2,))],
    )
    return pl.pallas_call(
        _combine_ln_kernel,
        grid_spec=grid_spec,
        out_shape=[jax.ShapeDtypeStruct((m, n), F32), jax.ShapeDtypeStruct((m, n), BF16)],
        compiler_params=_params(("arbitrary",)),
        name="combine_ln",
    )(dest, res, gate, g.reshape(1, n), b.reshape(1, n), y_pad)


def _dispatch_tables(idx):
    e_flat = idx.T.reshape(-1)
    onehot = (e_flat[:, None] == jnp.arange(N_EXPERTS, dtype=jnp.int32)[None, :]).astype(jnp.int32)
    csum = jnp.cumsum(onehot, axis=0)
    rank = jnp.take_along_axis(csum, e_flat[:, None], axis=1)[:, 0] - 1
    counts = csum[-1]
    padded = (counts + MOE_ROWS - 1) // MOE_ROWS * MOE_ROWS
    pad_end = jnp.cumsum(padded)
    dest = (pad_end - padded)[e_flat] + rank
    tok = jnp.arange(N_TOK * TOP_K, dtype=jnp.int32) // TOP_K
    src_tok = jnp.zeros((MOE_PAD_ROWS,), jnp.int32).at[dest].set(tok)
    starts = jnp.arange(MOE_BLOCKS, dtype=jnp.int32) * MOE_ROWS
    block_expert = jnp.minimum(jnp.sum((pad_end[None, :] <= starts[:, None]).astype(jnp.int32), axis=1),
                               N_EXPERTS - 1)
    n_used = (pad_end[-1:] // MOE_ROWS).astype(jnp.int32)
    ids = jnp.arange(N_EXPERTS, dtype=jnp.int32)
    later = jnp.where((ids[None, :] > ids[:, None]) & (counts[None, :] > 0), ids[None, :], N_EXPERTS)
    next_nonempty = jnp.min(later, axis=1)
    next_expert = jnp.where(next_nonempty < N_EXPERTS, next_nonempty, -1)[block_expert].astype(jnp.int32)
    return dest, src_tok, block_expert, n_used, next_expert


def _moe_layer(xf, router_w, w_gate_up, w_down, ln_g, ln_b, layer):
    wh, wl, bias = router_w
    idx, gate = _router(xf, wh, wl, bias)
    dest, src_tok, block_expert, n_used, next_expert = _dispatch_tables(idx)
    y_pad = _expert_ffn(xf, src_tok, block_expert, n_used, next_expert, w_gate_up, w_down, layer)
    return _combine_ln(xf, y_pad, dest, gate.T, ln_g, ln_b)


def _router_weights(w_router, router_bias):
    lane = np.arange(N_EXPERTS)
    perm = (lane % N_GROUPS) * EXPERTS_PER_GROUP + lane // N_GROUPS
    w = jnp.pad(w_router[:, perm], ((0, 0), (0, LANES - N_EXPERTS)))
    wh = w.astype(BF16)
    wl = (w - wh.astype(F32)).astype(BF16)
    return wh, wl, router_bias[perm].astype(F32).reshape(N_EXPERTS, 1)


def kernel(x, positions, w_in_hgrn, lb_param, g_norm_hgrn, w_out_hgrn, w_kv, b_kv, w_q_attn, b_q_attn, sinks, w_o_attn, b_o_attn, w_router, router_bias, w_gate_up, w_down, ln_mix_g, ln_mix_b, ln_ffn_g, ln_ffn_b):
    xf = x.reshape(N_TOK, D_MODEL)
    xb = xf.astype(BF16)
    pos = positions.reshape(N_TOK, 1).astype(F32)
    half = ROT_DIM // 2
    inv_freq = 1.0 / (ROPE_THETA ** (jnp.arange(half, dtype=F32) * 2.0 / ROT_DIM))
    lane = np.arange(LANES) % ATT_HEAD_DIM
    invf = jnp.where(lane < ROT_DIM, inv_freq[lane % half], 0.0).reshape(1, LANES).astype(F32)
    router_w = _router_weights(w_router, router_bias)
    zero_bias = jnp.zeros((D_MODEL,), F32)
    kv = None
    for layer in range(DEPTH):
        if layer < N_A_LAYERS:
            w_in = w_in_hgrn[layer]
            lbp = lb_param.astype(F32)
            part = lambda name, n: _hgrn_in_proj(
                xb, w_in[:, n * D_MODEL:(n + 1) * D_MODEL].astype(BF16), lbp, name, layer)
            (qs,) = part("q", 0)
            l2f, kg = part("f", 1)
            (v,) = part("v", 2)
            (gg,) = part("g", 3)
            o = _hgrn_scan(qs, kg, l2f, v, gg, g_norm_hgrn[layer])
            xf, xb = _matmul_ln(o, w_out_hgrn[layer].astype(BF16), zero_bias, xf,
                                ln_mix_g[layer], ln_mix_b[layer])
        else:
            if layer == N_A_LAYERS:
                kv = _proj_rotary(xb, w_kv.astype(BF16), b_kv, pos, invf,
                                  rot_chunks=ATT_KV_HEADS * ATT_HEAD_DIM // LANES, scale=1.0)
            j = layer - N_A_LAYERS
            q = _proj_rotary(xb, w_q_attn[j].astype(BF16), b_q_attn[j], pos, invf,
                             rot_chunks=D_MODEL // LANES, scale=ATT_HEAD_DIM ** -0.5, tm=256, tn=D_MODEL)
            o = _attention(q, kv, sinks[j].astype(F32))
            xf, xb = _matmul_ln(o, w_o_attn[j].astype(BF16), b_o_attn[j], xf,
                                ln_mix_g[layer], ln_mix_b[layer])
        xf, xb = _moe_layer(xf, router_w, w_gate_up, w_down, ln_ffn_g[layer], ln_ffn_b[layer], layer)
    return xf.reshape(BATCH, SEQ, D_MODEL)
```

```python
import functools
import math

import numpy as np
import jax
import jax.numpy as jnp
from jax import lax
from jax.experimental import pallas as pl
from jax.experimental.pallas import tpu as pltpu

D_MODEL = 2048
BATCH = 4
SEQ = 2048
DEPTH = 4
N_TOK = BATCH * SEQ

N_A_LAYERS = DEPTH // 2
N_B_LAYERS = DEPTH - N_A_LAYERS
HG_EXPAND = 128
HG_HEADS = D_MODEL // HG_EXPAND
HG_DK = HG_EXPAND
HG_DV = D_MODEL // HG_HEADS
LB_FLOOR = 1e-30
ATT_HEAD_DIM = 64
ATT_Q_HEADS = D_MODEL // ATT_HEAD_DIM
ATT_KV_HEADS = ATT_Q_HEADS // 8
ATT_GROUP = ATT_Q_HEADS // ATT_KV_HEADS
WINDOW = 128
ROT_DIM = ATT_HEAD_DIM // 4
ROPE_THETA = 500000.0
MASK_VALUE = -1e30
N_EXPERTS = 32
N_GROUPS = 8
EXPERTS_PER_GROUP = N_EXPERTS // N_GROUPS
TOP_K = 2
D_FF = (704 * D_MODEL) // 2048
ROUTE_MASK = -1e9
DEEPNORM_ALPHA = (2 * DEPTH) ** 0.25
LN_EPS = 1e-5
RMS_EPS = 1e-6

LANES = 128
MXU_WIDTH = 256
SCAN_CHUNK = 64
SCAN_LEVELS = 6
SCAN_HEADS = 4
SCAN_ROWS = 16
MOE_ROWS = 256
MOE_BLOCKS = (N_TOK * TOP_K + N_EXPERTS * (MOE_ROWS - 1) + MOE_ROWS - 1) // MOE_ROWS
MOE_PAD_ROWS = MOE_BLOCKS * MOE_ROWS
WEIGHT_DMA_THREAD = 0
ROW_DMA_THREAD = 1
CAST_ROWS = 32
COMBINE_ROWS = 32
VMEM_LIMIT = 56 * 1024 * 1024
EXPERT_VMEM_LIMIT = 58 * 1024 * 1024

F32 = jnp.float32
BF16 = jnp.bfloat16


def _dot(a, b):
    return jnp.dot(a, b, preferred_element_type=F32)


def _dot_nt(a, b):
    return lax.dot_general(a, b, (((1,), (1,)), ((), ())), preferred_element_type=F32)


def _dot_tn(a, b):
    return lax.dot_general(a, b, (((0,), (0,)), ((), ())), preferred_element_type=F32)


def _sigmoid(x):
    return 1.0 / (1.0 + jnp.exp(-x))


def _params(sem, limit=VMEM_LIMIT):
    return pltpu.CompilerParams(dimension_semantics=sem, vmem_limit_bytes=limit)


def _hgrn_in_kernel(a_ref, w_ref, lbp_ref, *o_refs, part, layer):
    acc = _dot(a_ref[...], w_ref[...])
    if part == "q":
        o_refs[0][...] = (acc * _sigmoid(acc) * (HG_DK ** -0.5)).astype(BF16)
    elif part == "v":
        o_refs[0][...] = acc.astype(BF16)
    elif part == "g":
        o_refs[0][...] = (acc * _sigmoid(acc)).astype(BF16)
    else:
        lbp = lbp_ref[...]
        e = jnp.exp(lbp - jnp.max(lbp, axis=0, keepdims=True))
        sm = e / jnp.sum(e, axis=0, keepdims=True)
        lb = jnp.zeros_like(sm[0:1])
        for r in range(1, layer + 1):
            lb = lb + sm[r:r + 1]
        ez = jnp.exp(-jnp.abs(acc))
        rz = 1.0 / (1.0 + ez)
        sig_pos = jnp.where(acc >= 0, rz, ez * rz)
        sig_neg = jnp.where(acc >= 0, ez * rz, rz)
        forget = jnp.maximum(lb, LB_FLOOR) + (1.0 - lb) * sig_pos
        o_refs[0][...] = jnp.log(forget) * (1.0 / math.log(2.0))
        o_refs[1][...] = ((1.0 - lb) * sig_neg).astype(BF16)


def _hgrn_in_proj(a, w, lb_param, part, layer, tm=512, tn=1024):
    m, k = a.shape
    n = w.shape[1]
    blk = pl.BlockSpec((tm, tn), lambda i, j: (i, j))
    if part == "f":
        out_specs = [blk, blk]
        out_shape = [jax.ShapeDtypeStruct((m, n), F32), jax.ShapeDtypeStruct((m, n), BF16)]
    else:
        out_specs = [blk]
        out_shape = [jax.ShapeDtypeStruct((m, n), BF16)]
    return pl.pallas_call(
        functools.partial(_hgrn_in_kernel, part=part, layer=layer),
        grid=(m // tm, n // tn),
        in_specs=[pl.BlockSpec((tm, k), lambda i, j: (i, 0)),
                  pl.BlockSpec((k, tn), lambda i, j: (0, j)),
                  pl.BlockSpec((N_A_LAYERS, tn), lambda i, j: (0, j))],
        out_specs=out_specs,
        out_shape=out_shape,
        compiler_params=_params(("parallel", "parallel")),
        name="hgrn_in_" + part,
    )(a, w, lb_param)


def _scan_constants():
    c = SCAN_CHUNK
    t = np.arange(c)[None, :]
    i = np.arange(c)[:, None]
    blocks = [(t <= i), (t > i)]
    masks = [np.eye(c, dtype=np.float32)]
    j = np.arange(c)[None, :]
    for lvl in range(1, SCAN_LEVELS + 1):
        s = 1 << lvl
        h = s // 2
        m = (i // s) * s + h - 1
        query_role = (i % s) >= h
        blocks.append(np.where(query_role, (t > m) & (t <= i), (t > i) & (t <= m)))
        masks.append((((i // s) == (j // s)) & query_role & ((j % s) < h)).astype(np.float32))
    a = np.concatenate(blocks, axis=0).astype(np.float32)
    return np.concatenate([a, a], axis=1), np.stack(masks)


def _query_row_blocks(lvl):
    size = 1 << lvl
    nblk = SCAN_CHUNK // SCAN_ROWS
    if size <= SCAN_ROWS:
        return list(range(nblk))
    return [r for r in range(nblk) if (r * SCAN_ROWS) % size >= size // 2]


def _hgrn_scan_kernel(qs_ref, k_ref, l2f_ref, v_ref, gg_ref, gn_ref, sums_ref, mask_ref, o_ref, st_ref, ex_ref):
    c = SCAN_CHUNK
    rb = SCAN_ROWS
    nblk = c // rb
    gn = gn_ref[...]
    st_ref[...] = jnp.zeros_like(st_ref)

    def decay_stage(r0, buf):
        for pair in range(SCAN_HEADS // 2):
            cols = slice(pair * 2 * LANES, (pair + 1) * 2 * LANES)
            l2f = l2f_ref[pl.ds(r0, c), cols]
            hi = l2f.astype(BF16)
            lo = (l2f - hi.astype(F32)).astype(BF16)
            ex_ref[buf, :, cols] = jnp.exp2(_dot(sums_ref[...], jnp.concatenate([hi, lo], axis=0)))

    every = list(range(nblk))

    def decays(buf, hd, block, rows):
        cols = slice(hd * LANES, (hd + 1) * LANES)
        return jnp.concatenate(
            [ex_ref[buf, block * c + r * rb:block * c + (r + 1) * rb, cols] for r in rows], axis=0)

    def intra_scores(r0, hd, buf):
        cols = slice(hd * LANES, (hd + 1) * LANES)
        qs = qs_ref[pl.ds(r0, c), cols]
        k = k_ref[pl.ds(r0, c), cols]
        qs32 = qs.astype(F32)
        k32 = k.astype(F32)
        diag = _dot_nt(qs, k)
        sc = [diag[r * rb:(r + 1) * rb] * mask_ref[0, r * rb:(r + 1) * rb, :] for r in every]
        for lvl in range(1, SCAN_LEVELS + 1):
            rows = _query_row_blocks(lvl)
            ql = jnp.concatenate([qs32[r * rb:(r + 1) * rb] for r in rows], axis=0) * decays(buf, hd, lvl + 1, rows)
            part = _dot_nt(ql.astype(BF16), (k32 * decays(buf, hd, lvl + 1, every)).astype(BF16))
            for n, r in enumerate(rows):
                sc[r] = sc[r] + part[n * rb:(n + 1) * rb] * mask_ref[lvl, r * rb:(r + 1) * rb, :]
        return jnp.concatenate(sc, axis=0).astype(BF16)

    def state_step(r0, hd, buf, scores):
        cols = slice(hd * LANES, (hd + 1) * LANES)
        qs32 = qs_ref[pl.ds(r0, c), cols].astype(F32)
        k32 = k_ref[pl.ds(r0, c), cols].astype(F32)
        v = v_ref[pl.ds(r0, c), cols]
        eb = decays(buf, hd, 0, every)
        ek = decays(buf, hd, 1, every)
        st = st_ref[hd]
        o = _dot_nt((qs32 * eb).astype(BF16), st.astype(BF16)) + _dot(scores, v)
        st_ref[hd] = st * eb[c - 1:c, :] + _dot_tn(v, (k32 * ek).astype(BF16))
        on = o * lax.rsqrt(jnp.mean(o * o, axis=-1, keepdims=True) + RMS_EPS)
        o_ref[pl.ds(r0, c), cols] = (on * gn * gg_ref[pl.ds(r0, c), cols].astype(F32)).astype(o_ref.dtype)

    decay_stage(0, 0)

    def chunk_pair(cp, carry):
        for half in range(2):
            r0 = pl.multiple_of(cp * (2 * c) + half * c, c)
            scores = [intra_scores(r0, hd, half) for hd in range(SCAN_HEADS)]
            decay_stage(pl.multiple_of(jnp.minimum(r0 + c, SEQ - c), c), 1 - half)
            for hd in range(SCAN_HEADS):
                state_step(r0, hd, half, scores[hd])
        return carry

    lax.fori_loop(0, SEQ // (2 * c), chunk_pair, 0)


def _hgrn_scan(qs, k, l2f, v, gg, g_norm):
    sums_np, masks_np = _scan_constants()
    sums_m = jnp.asarray(sums_np, BF16)
    masks = jnp.asarray(masks_np, F32)
    width = SCAN_HEADS * LANES
    blk = pl.BlockSpec((SEQ, width), lambda b, hg: (b, hg))
    return pl.pallas_call(
        _hgrn_scan_kernel,
        grid=(BATCH, HG_HEADS // SCAN_HEADS),
        in_specs=[blk, blk, blk, blk, blk,
                  pl.BlockSpec((1, LANES), lambda b, hg: (0, 0)),
                  pl.BlockSpec(sums_m.shape, lambda b, hg: (0, 0)),
                  pl.BlockSpec(masks.shape, lambda b, hg: (0, 0, 0))],
        out_specs=blk,
        out_shape=jax.ShapeDtypeStruct((N_TOK, D_MODEL), BF16),
        scratch_shapes=[pltpu.VMEM((SCAN_HEADS, HG_DV, HG_DK), F32),
                        pltpu.VMEM((2, (SCAN_LEVELS + 2) * SCAN_CHUNK, width), F32)],
        compiler_params=_params(("parallel", "parallel")),
        name="hgrn_scan",
    )(qs, k, l2f, v, gg, g_norm.reshape(1, HG_DV), sums_m, masks)


def _layer_norm(y, g, b):
    mu = jnp.mean(y, axis=-1, keepdims=True)
    d = y - mu
    var = jnp.mean(d * d, axis=-1, keepdims=True)
    return d * lax.rsqrt(var + LN_EPS) * g + b


def _matmul_ln_kernel(a_ref, w_ref, bias_ref, res_ref, g_ref, b_ref, of_ref, ob_ref):
    mix = _dot(a_ref[...], w_ref[...]) + bias_ref[...]
    out = _layer_norm(DEEPNORM_ALPHA * res_ref[...] + mix, g_ref[...], b_ref[...])
    of_ref[...] = out
    ob_ref[...] = out.astype(BF16)


def _matmul_ln(a, w, bias, res, g, b, tm=256):
    m, k = a.shape
    n = w.shape[1]
    row = lambda width: pl.BlockSpec((1, width), lambda i: (0, 0))
    return pl.pallas_call(
        _matmul_ln_kernel,
        grid=(m // tm,),
        in_specs=[pl.BlockSpec((tm, k), lambda i: (i, 0)),
                  pl.BlockSpec((k, n), lambda i: (0, 0)),
                  row(n),
                  pl.BlockSpec((tm, n), lambda i: (i, 0)),
                  row(n), row(n)],
        out_specs=[pl.BlockSpec((tm, n), lambda i: (i, 0)), pl.BlockSpec((tm, n), lambda i: (i, 0))],
        out_shape=[jax.ShapeDtypeStruct((m, n), F32), jax.ShapeDtypeStruct((m, n), BF16)],
        compiler_params=_params(("parallel",)),
        name="out_proj_ln",
    )(a, w, bias.reshape(1, n), res, g.reshape(1, n), b.reshape(1, n))


def _proj_rotary_kernel(a_ref, w_ref, bias_ref, pos_ref, invf_ref, o_ref, *, rot_chunks, scale):
    acc = _dot(a_ref[...], w_ref[...]) + bias_ref[...]
    ang = pos_ref[...] * invf_ref[...]
    lane = lax.broadcasted_iota(jnp.int32, (1, LANES), 1) % ATT_HEAD_DIM
    half = ROT_DIM // 2
    cos = jnp.cos(ang)
    sin = jnp.sin(ang)
    c_keep = jnp.where(lane < ROT_DIM, cos, 1.0)
    s_from_left = jnp.where((lane >= half) & (lane < ROT_DIM), sin, 0.0)
    s_from_right = jnp.where(lane < half, -sin, 0.0)
    for j in range(acc.shape[1] // LANES):
        t = acc[:, j * LANES:(j + 1) * LANES]
        if j < rot_chunks:
            t = (t * c_keep + pltpu.roll(t, half, 1) * s_from_left
                 + pltpu.roll(t, LANES - half, 1) * s_from_right)
        o_ref[:, j * LANES:(j + 1) * LANES] = (t * scale).astype(o_ref.dtype)


def _proj_rotary(a, w, bias, pos, invf, rot_chunks, scale, tm=512, tn=512):
    m, k = a.shape
    n = w.shape[1]
    return pl.pallas_call(
        functools.partial(_proj_rotary_kernel, rot_chunks=rot_chunks, scale=scale),
        grid=(m // tm, n // tn),
        in_specs=[pl.BlockSpec((tm, k), lambda i, j: (i, 0)),
                  pl.BlockSpec((k, tn), lambda i, j: (0, j)),
                  pl.BlockSpec((1, tn), lambda i, j: (0, j)),
                  pl.BlockSpec((tm, 1), lambda i, j: (i, 0)),
                  pl.BlockSpec((1, LANES), lambda i, j: (0, 0))],
        out_specs=pl.BlockSpec((tm, tn), lambda i, j: (i, j)),
        out_shape=jax.ShapeDtypeStruct((m, n), BF16),
        compiler_params=_params(("parallel", "parallel")),
        name="proj_rotary",
    )(a, w, bias.reshape(1, n), pos, invf)


def _attn_kernel(sink_ref, q_ref, kp_ref, kc_ref, vp_ref, vc_ref, o_ref):
    nb = pl.program_id(1)
    w = WINDOW
    hd = ATT_HEAD_DIM
    k = jnp.concatenate([kp_ref[...], kc_ref[...]], axis=0)
    v = jnp.concatenate([vp_ref[...], vc_ref[...]], axis=0)
    qi = lax.broadcasted_iota(jnp.int32, (w, 2 * w), 0) + w
    ki = lax.broadcasted_iota(jnp.int32, (w, 2 * w), 1)
    rel = qi - ki
    valid = (rel >= 0) & (rel < w) & ((ki >= w) | (nb > 0))
    for j in range(ATT_KV_HEADS):
        kj = k[:, j * hd:(j + 1) * hd]
        vj = v[:, j * hd:(j + 1) * hd]
        heads = [j * ATT_GROUP + gq for gq in range(ATT_GROUP)]
        qw = q_ref[:, j * ATT_GROUP * hd:(j + 1) * ATT_GROUP * hd]
        qj = jnp.concatenate([qw[:, gq * hd:(gq + 1) * hd] for gq in range(ATT_GROUP)], axis=0)
        s_all = _dot_nt(qj, kj)
        outs = []
        for gq, h in enumerate(heads):
            s = jnp.where(valid, s_all[gq * w:(gq + 1) * w], MASK_VALUE)
            sink = sink_ref[h]
            m = jnp.maximum(jnp.max(s, axis=-1, keepdims=True), sink)
            p = jnp.exp(s - m)
            denom = jnp.sum(p, axis=-1, keepdims=True) + jnp.exp(sink - m)
            outs.append(_dot(p.astype(BF16), vj) / denom)
        for gq in range(0, ATT_GROUP, 2):
            h = heads[gq]
            o_ref[:, h * hd:(h + 2) * hd] = jnp.concatenate(outs[gq:gq + 2], axis=1).astype(o_ref.dtype)


def _attention(q, kv, sinks):
    nblk = SEQ // WINDOW
    kvw = ATT_KV_HEADS * ATT_HEAD_DIM
    cur = lambda col: pl.BlockSpec((WINDOW, kvw), lambda b, n, s: (b * nblk + n, col))
    prev = lambda col: pl.BlockSpec((WINDOW, kvw), lambda b, n, s: (b * nblk + jnp.maximum(n - 1, 0), col))
    grid_spec = pltpu.PrefetchScalarGridSpec(
        num_scalar_prefetch=1,
        grid=(BATCH, nblk),
        in_specs=[pl.BlockSpec((WINDOW, D_MODEL), lambda b, n, s: (b * nblk + n, 0)),
                  prev(0), cur(0), prev(1), cur(1)],
        out_specs=pl.BlockSpec((WINDOW, D_MODEL), lambda b, n, s: (b * nblk + n, 0)),
    )
    return pl.pallas_call(
        _attn_kernel,
        grid_spec=grid_spec,
        out_shape=jax.ShapeDtypeStruct((N_TOK, D_MODEL), BF16),
        compiler_params=_params(("parallel", "parallel")),
        name="swa_attention",
    )(sinks, q, kv, kv, kv, kv)


def _router_kernel(x_ref, wh_ref, wl_ref, bias_ref, idx_ref, gate_ref):
    x = x_ref[...]
    xh = x.astype(BF16)
    xl = (x - xh.astype(F32)).astype(BF16)
    wh = wh_ref[...]
    logits = _dot(xh, wh) + _dot(xl, wh) + _dot(xh, wl_ref[...])
    lt = logits.T[0:N_EXPERTS]
    scores = _sigmoid(lt)
    biased = scores + bias_ref[...]
    g = N_GROUPS
    a, b, c, d = (biased[p * g:(p + 1) * g] for p in range(EXPERTS_PER_GROUP))
    group_score = jnp.maximum(jnp.maximum(a + b, c + d), jnp.maximum(a, b) + jnp.maximum(c, d))
    gi = lax.broadcasted_iota(jnp.int32, group_score.shape, 0)
    best = jnp.min(jnp.where(group_score == jnp.max(group_score, axis=0, keepdims=True), gi, g),
                   axis=0, keepdims=True)
    row = lax.broadcasted_iota(jnp.int32, biased.shape, 0)
    expert = (row % g) * EXPERTS_PER_GROUP + row // g
    masked = jnp.where((row % g) == best, biased, ROUTE_MASK)

    def take_top(vals):
        top = jnp.max(vals, axis=0, keepdims=True)
        return jnp.min(jnp.where(vals == top, expert, N_EXPERTS), axis=0, keepdims=True)

    first = take_top(masked)
    second = take_top(jnp.where(expert == first, -jnp.inf, masked))
    w1 = jnp.sum(jnp.where(expert == first, scores, 0.0), axis=0, keepdims=True)
    w2 = jnp.sum(jnp.where(expert == second, scores, 0.0), axis=0, keepdims=True)
    idx_ref[0:1, :] = first
    idx_ref[1:2, :] = second
    gate_ref[0:1, :] = w1 / (w1 + w2)
    gate_ref[1:2, :] = w2 / (w1 + w2)


def _router(x, wh, wl, bias, tm=512):
    m, k = x.shape
    return pl.pallas_call(
        _router_kernel,
        grid=(m // tm,),
        in_specs=[pl.BlockSpec((tm, k), lambda i: (i, 0)),
                  pl.BlockSpec((k, LANES), lambda i: (0, 0)),
                  pl.BlockSpec((k, LANES), lambda i: (0, 0)),
                  pl.BlockSpec((N_EXPERTS, 1), lambda i: (0, 0))],
        out_specs=[pl.BlockSpec((TOP_K, tm), lambda i: (0, i)), pl.BlockSpec((TOP_K, tm), lambda i: (0, i))],
        out_shape=[jax.ShapeDtypeStruct((TOP_K, m), jnp.int32), jax.ShapeDtypeStruct((TOP_K, m), F32)],
        compiler_params=_params(("parallel",)),
        name="router",
    )(x, wh, wl, bias)


def _expert_kernel(be_ref, nused_ref, wslot_ref, next1_ref, next2_ref, src_ref, x_hbm, wgu_hbm, wd_hbm, y_ref,
                   wgu_f, wd_f, wgu_b, wd_b, xbuf, wsem, xsem, *, layer):
    i = pl.program_id(0)
    n_used = nused_ref[0]
    e = be_ref[i]
    used = i < n_used
    first = (i == 0) | (e != be_ref[jnp.maximum(i - 1, 0)])
    slot = i % 2
    wslot = wslot_ref[i]

    def weight_copies(expert, stage):
        return (pltpu.make_async_copy(wgu_hbm.at[layer, expert], wgu_f.at[stage], wsem.at[stage, 0]),
                pltpu.make_async_copy(wd_hbm.at[layer, expert], wd_f.at[stage], wsem.at[stage, 1]))

    def row_copy(tok, to_slot, r):
        return pltpu.make_async_copy(x_hbm.at[pl.ds(tok, 1)], xbuf.at[to_slot, pl.ds(r, 1)], xsem.at[to_slot])

    def start_rows(block, to_slot):
        base = block * MOE_ROWS
        for r in range(MOE_ROWS):
            row_copy(src_ref[base + r], to_slot, r).start(priority=ROW_DMA_THREAD)

    def wait_rows(of_slot):
        for r in range(MOE_ROWS):
            row_copy(0, of_slot, r).wait()

    @pl.when(i == 0)
    def _():
        for cp in weight_copies(e, 0):
            cp.start(priority=WEIGHT_DMA_THREAD)
        nxt = next1_ref[0]

        @pl.when(nxt >= 0)
        def _():
            for cp in weight_copies(nxt, 1):
                cp.start(priority=WEIGHT_DMA_THREAD)

        start_rows(0, 0)

    @pl.when(used & first)
    def _():
        for cp in weight_copies(e, wslot):
            cp.wait()
        for f32_ref, bf16_ref in ((wgu_f, wgu_b), (wd_f, wd_b)):
            def cast_rows(c, carry, f32_ref=f32_ref, bf16_ref=bf16_ref):
                rows = pl.ds(pl.multiple_of(c * CAST_ROWS, CAST_ROWS), CAST_ROWS)
                bf16_ref[rows, :] = f32_ref[wslot, rows, :].astype(BF16)
                return carry

            lax.fori_loop(0, f32_ref.shape[1] // CAST_ROWS, cast_rows, 0)
        after_next = next2_ref[i]

        @pl.when(after_next >= 0)
        def _():
            for cp in weight_copies(after_next, wslot):
                cp.start(priority=WEIGHT_DMA_THREAD)

    @pl.when(used)
    def _():
        wait_rows(slot)
        base = jnp.minimum(i + 1, n_used - 1) * MOE_ROWS
        x = xbuf[slot].astype(BF16)
        width = 2 * MXU_WIDTH
        n_chunks = -(-2 * D_FF // width)
        per_chunk = -(-MOE_ROWS // n_chunks)
        parts = []
        for c in range(n_chunks):
            parts.append(_dot(x, wgu_b[:, c * width:min((c + 1) * width, 2 * D_FF)]))
            for r in range(c * per_chunk, min((c + 1) * per_chunk, MOE_ROWS)):
                row_copy(src_ref[base + r], 1 - slot, r).start(priority=ROW_DMA_THREAD)
        h = jnp.concatenate(parts, axis=1)
        hg = h[:, :D_FF]
        hu = h[:, D_FF:]
        act = (hg * _sigmoid(hg) * hu).astype(BF16)
        y_ref[...] = _dot(act, wd_b[...])

    @pl.when(used & (i == n_used - 1))
    def _():
        wait_rows(1 - slot)

    @pl.when(jnp.logical_not(used))
    def _():
        y_ref[...] = jnp.zeros_like(y_ref)


def _expert_ffn(x, src_tok, block_expert, n_used, weight_plan, w_gate_up, w_down, layer):
    any_space = pl.BlockSpec(memory_space=pl.ANY)
    grid_spec = pltpu.PrefetchScalarGridSpec(
        num_scalar_prefetch=6,
        grid=(MOE_BLOCKS,),
        in_specs=[any_space, any_space, any_space],
        out_specs=pl.BlockSpec((MOE_ROWS, D_MODEL), lambda i, *_: (i, 0)),
        scratch_shapes=[pltpu.VMEM((2, D_MODEL, 2 * D_FF), F32), pltpu.VMEM((2, D_FF, D_MODEL), F32),
                        pltpu.VMEM((D_MODEL, 2 * D_FF), BF16), pltpu.VMEM((D_FF, D_MODEL), BF16),
                        pltpu.VMEM((2, MOE_ROWS, D_MODEL), F32),
                        pltpu.SemaphoreType.DMA((2, 2)), pltpu.SemaphoreType.DMA((2,))],
    )
    return pl.pallas_call(
        functools.partial(_expert_kernel, layer=layer),
        grid_spec=grid_spec,
        out_shape=jax.ShapeDtypeStruct((MOE_PAD_ROWS, D_MODEL), F32),
        compiler_params=_params(("arbitrary",), EXPERT_VMEM_LIMIT),
        name="expert_ffn",
    )(block_expert, n_used, *weight_plan, src_tok, x, w_gate_up, w_down)


def _combine_ln_kernel(dest_ref, res_ref, gate_ref, g_ref, b_ref, y_hbm, of_ref, ob_ref, ybuf, ysem):
    i = pl.program_id(0)
    slot = i % 2
    tm = res_ref.shape[0]

    def row_copy(row, to_slot, k, r):
        return pltpu.make_async_copy(y_hbm.at[pl.ds(row, 1)], ybuf.at[to_slot, k, pl.ds(r, 1)], ysem.at[to_slot])

    def start_rows(base, to_slot, rows):
        for r in rows:
            for k in range(TOP_K):
                row_copy(dest_ref[base + r * TOP_K + k], to_slot, k, r).start()

    def wait_rows(of_slot):
        for r in range(tm):
            for k in range(TOP_K):
                row_copy(0, of_slot, k, r).wait()

    @pl.when(i == 0)
    def _():
        start_rows(0, 0, range(tm))

    wait_rows(slot)
    last = pl.num_programs(0) - 1
    base = jnp.minimum(i + 1, last) * (tm * TOP_K)
    g = g_ref[...]
    b = b_ref[...]
    for c in range(tm // COMBINE_ROWS):
        rows = slice(c * COMBINE_ROWS, (c + 1) * COMBINE_ROWS)
        gate = gate_ref[rows, :]
        ffn = ybuf[slot, 0, rows, :] * gate[:, 0:1] + ybuf[slot, 1, rows, :] * gate[:, 1:2]
        out = _layer_norm(DEEPNORM_ALPHA * res_ref[rows, :] + ffn, g, b)
        of_ref[rows, :] = out
        ob_ref[rows, :] = out.astype(BF16)
        start_rows(base, 1 - slot, range(c * COMBINE_ROWS, (c + 1) * COMBINE_ROWS))

    @pl.when(i == last)
    def _():
        wait_rows(1 - slot)


def _combine_ln(res, y_pad, dest, gate, g, b, tm=256):
    m, n = res.shape
    blk = pl.BlockSpec((tm, n), lambda i, d: (i, 0))
    row = pl.BlockSpec((1, n), lambda i, d: (0, 0))
    grid_spec = pltpu.PrefetchScalarGridSpec(
        num_scalar_prefetch=1,
        grid=(m // tm,),
        in_specs=[blk, pl.BlockSpec((tm, TOP_K), lambda i, d: (i, 0)), row, row,
                  pl.BlockSpec(memory_space=pl.ANY)],
        out_specs=[blk, blk],
        scratch_shapes=[pltpu.VMEM((2, TOP_K, tm, n), F32), pltpu.SemaphoreType.DMA((2,))],
    )
    return pl.pallas_call(
        _combine_ln_kernel,
        grid_spec=grid_spec,
        out_shape=[jax.ShapeDtypeStruct((m, n), F32), jax.ShapeDtypeStruct((m, n), BF16)],
        compiler_params=_params(("arbitrary",)),
        name="combine_ln",
    )(dest, res, gate, g.reshape(1, n), b.reshape(1, n), y_pad)


def _dispatch_tables(idx):
    e_flat = idx.T.reshape(-1)
    onehot = (e_flat[:, None] == jnp.arange(N_EXPERTS, dtype=jnp.int32)[None, :]).astype(jnp.int32)
    csum = jnp.cumsum(onehot, axis=0)
    rank = jnp.take_along_axis(csum, e_flat[:, None], axis=1)[:, 0] - 1
    counts = csum[-1]
    padded = (counts + MOE_ROWS - 1) // MOE_ROWS * MOE_ROWS
    pad_end = jnp.cumsum(padded)
    dest = (pad_end - padded)[e_flat] + rank
    tok = jnp.arange(N_TOK * TOP_K, dtype=jnp.int32) // TOP_K
    src_tok = jnp.zeros((MOE_PAD_ROWS,), jnp.int32).at[dest].set(tok)
    starts = jnp.arange(MOE_BLOCKS, dtype=jnp.int32) * MOE_ROWS
    block_expert = jnp.minimum(jnp.sum((pad_end[None, :] <= starts[:, None]).astype(jnp.int32), axis=1),
                               N_EXPERTS - 1)
    n_used = (pad_end[-1:] // MOE_ROWS).astype(jnp.int32)
    ids = jnp.arange(N_EXPERTS, dtype=jnp.int32)
    has_rows = counts > 0
    later = jnp.where((ids[None, :] > ids[:, None]) & has_rows[None, :], ids[None, :], N_EXPERTS)
    next1 = jnp.concatenate([jnp.min(later, axis=1), jnp.full((1,), N_EXPERTS, jnp.int32)])
    next2 = next1[next1]
    ordinal = jnp.cumsum(has_rows.astype(jnp.int32)) - has_rows.astype(jnp.int32)
    as_plan = lambda t: jnp.where(t < N_EXPERTS, t, -1)[block_expert].astype(jnp.int32)
    weight_plan = ((ordinal % 2)[block_expert].astype(jnp.int32), as_plan(next1[:N_EXPERTS]), as_plan(next2[:N_EXPERTS]))
    return dest, src_tok, block_expert, n_used, weight_plan


def _moe_layer(xf, router_w, w_gate_up, w_down, ln_g, ln_b, layer):
    wh, wl, bias = router_w
    idx, gate = _router(xf, wh, wl, bias)
    dest, src_tok, block_expert, n_used, weight_plan = _dispatch_tables(idx)
    y_pad = _expert_ffn(xf, src_tok, block_expert, n_used, weight_plan, w_gate_up, w_down, layer)
    return _combine_ln(xf, y_pad, dest, gate.T, ln_g, ln_b)


def _router_weights(w_router, router_bias):
    lane = np.arange(N_EXPERTS)
    perm = (lane % N_GROUPS) * EXPERTS_PER_GROUP + lane // N_GROUPS
    w = jnp.pad(w_router[:, perm], ((0, 0), (0, LANES - N_EXPERTS)))
    wh = w.astype(BF16)
    wl = (w - wh.astype(F32)).astype(BF16)
    return wh, wl, router_bias[perm].astype(F32).reshape(N_EXPERTS, 1)


def kernel(x, positions, w_in_hgrn, lb_param, g_norm_hgrn, w_out_hgrn, w_kv, b_kv, w_q_attn, b_q_attn, sinks, w_o_attn, b_o_attn, w_router, router_bias, w_gate_up, w_down, ln_mix_g, ln_mix_b, ln_ffn_g, ln_ffn_b):
    xf = x.reshape(N_TOK, D_MODEL)
    xb = xf.astype(BF16)
    pos = positions.reshape(N_TOK, 1).astype(F32)
    half = ROT_DIM // 2
    inv_freq = 1.0 / (ROPE_THETA ** (jnp.arange(half, dtype=F32) * 2.0 / ROT_DIM))
    lane = np.arange(LANES) % ATT_HEAD_DIM
    invf = jnp.where(lane < ROT_DIM, inv_freq[lane % half], 0.0).reshape(1, LANES).astype(F32)
    router_w = _router_weights(w_router, router_bias)
    zero_bias = jnp.zeros((D_MODEL,), F32)
    kv = None
    for layer in range(DEPTH):
        if layer < N_A_LAYERS:
            w_in = w_in_hgrn[layer]
            lbp = lb_param.astype(F32)
            part = lambda name, n: _hgrn_in_proj(
                xb, w_in[:, n * D_MODEL:(n + 1) * D_MODEL].astype(BF16), lbp, name, layer)
            (qs,) = part("q", 0)
            l2f, kg = part("f", 1)
            (v,) = part("v", 2)
            (gg,) = part("g", 3)
            o = _hgrn_scan(qs, kg, l2f, v, gg, g_norm_hgrn[layer])
            xf, xb = _matmul_ln(o, w_out_hgrn[layer].astype(BF16), zero_bias, xf,
                                ln_mix_g[layer], ln_mix_b[layer])
        else:
            if layer == N_A_LAYERS:
                kv = _proj_rotary(xb, w_kv.astype(BF16), b_kv, pos, invf,
                                  rot_chunks=ATT_KV_HEADS * ATT_HEAD_DIM // LANES, scale=1.0)
            j = layer - N_A_LAYERS
            q = _proj_rotary(xb, w_q_attn[j].astype(BF16), b_q_attn[j], pos, invf,
                             rot_chunks=D_MODEL // LANES, scale=ATT_HEAD_DIM ** -0.5, tm=256, tn=D_MODEL)
            o = _attention(q, kv, sinks[j].astype(F32))
            xf, xb = _matmul_ln(o, w_o_attn[j].astype(BF16), b_o_attn[j], xf,
                                ln_mix_g[layer], ln_mix_b[layer])
        xf, xb = _moe_layer(xf, router_w, w_gate_up, w_down, ln_ffn_g[layer], ln_ffn_b[layer], layer)
    return xf.reshape(BATCH, SEQ, D_MODEL)
```

```python
import functools
import math

import numpy as np
import jax
import jax.numpy as jnp
from jax import lax
from jax.experimental import pallas as pl
from jax.experimental.pallas import tpu as pltpu

D_MODEL = 2048
BATCH = 4
SEQ = 2048
DEPTH = 4
N_TOK = BATCH * SEQ

N_A_LAYERS = DEPTH // 2
N_B_LAYERS = DEPTH - N_A_LAYERS
HG_EXPAND = 128
HG_HEADS = D_MODEL // HG_EXPAND
HG_DK = HG_EXPAND
HG_DV = D_MODEL // HG_HEADS
LB_FLOOR = 1e-30
ATT_HEAD_DIM = 64
ATT_Q_HEADS = D_MODEL // ATT_HEAD_DIM
ATT_KV_HEADS = ATT_Q_HEADS // 8
ATT_GROUP = ATT_Q_HEADS // ATT_KV_HEADS
WINDOW = 128
ROT_DIM = ATT_HEAD_DIM // 4
ROPE_THETA = 500000.0
MASK_VALUE = -1e30
N_EXPERTS = 32
N_GROUPS = 8
EXPERTS_PER_GROUP = N_EXPERTS // N_GROUPS
TOP_K = 2
D_FF = (704 * D_MODEL) // 2048
ROUTE_MASK = -1e9
DEEPNORM_ALPHA = (2 * DEPTH) ** 0.25
LN_EPS = 1e-5
RMS_EPS = 1e-6

LANES = 128
SCAN_CHUNK = 64
SCAN_LEVELS = 6
SCAN_HEADS = 4
SCAN_ROWS = 16
MOE_ROWS = 256
MOE_BLOCKS = (N_TOK * TOP_K + N_EXPERTS * (MOE_ROWS - 1) + MOE_ROWS - 1) // MOE_ROWS
MOE_PAD_ROWS = MOE_BLOCKS * MOE_ROWS
WEIGHT_DMA_THREAD = 1
COMBINE_ROWS = 32
VMEM_LIMIT = 56 * 1024 * 1024

F32 = jnp.float32
BF16 = jnp.bfloat16


def _dot(a, b):
    return jnp.dot(a, b, preferred_element_type=F32)


def _dot_nt(a, b):
    return lax.dot_general(a, b, (((1,), (1,)), ((), ())), preferred_element_type=F32)


def _dot_tn(a, b):
    return lax.dot_general(a, b, (((0,), (0,)), ((), ())), preferred_element_type=F32)


def _sigmoid(x):
    return 1.0 / (1.0 + jnp.exp(-x))


def _params(sem, limit=VMEM_LIMIT):
    return pltpu.CompilerParams(dimension_semantics=sem, vmem_limit_bytes=limit)


def _hgrn_in_kernel(a_ref, w_ref, lbp_ref, *o_refs, part, layer):
    acc = _dot(a_ref[...], w_ref[...])
    if part == "q":
        o_refs[0][...] = (acc * _sigmoid(acc) * (HG_DK ** -0.5)).astype(BF16)
    elif part == "v":
        o_refs[0][...] = acc.astype(BF16)
    elif part == "g":
        o_refs[0][...] = (acc * _sigmoid(acc)).astype(BF16)
    else:
        lbp = lbp_ref[...]
        e = jnp.exp(lbp - jnp.max(lbp, axis=0, keepdims=True))
        sm = e / jnp.sum(e, axis=0, keepdims=True)
        lb = jnp.zeros_like(sm[0:1])
        for r in range(1, layer + 1):
            lb = lb + sm[r:r + 1]
        ez = jnp.exp(-jnp.abs(acc))
        rz = 1.0 / (1.0 + ez)
        sig_pos = jnp.where(acc >= 0, rz, ez * rz)
        sig_neg = jnp.where(acc >= 0, ez * rz, rz)
        forget = jnp.maximum(lb, LB_FLOOR) + (1.0 - lb) * sig_pos
        o_refs[0][...] = jnp.log(forget) * (1.0 / math.log(2.0))
        o_refs[1][...] = ((1.0 - lb) * sig_neg).astype(BF16)


def _hgrn_in_proj(a, w, lb_param, part, layer, tm=512, tn=1024):
    m, k = a.shape
    n = w.shape[1]
    blk = pl.BlockSpec((tm, tn), lambda i, j: (i, j))
    if part == "f":
        out_specs = [blk, blk]
        out_shape = [jax.ShapeDtypeStruct((m, n), F32), jax.ShapeDtypeStruct((m, n), BF16)]
    else:
        out_specs = [blk]
        out_shape = [jax.ShapeDtypeStruct((m, n), BF16)]
    return pl.pallas_call(
        functools.partial(_hgrn_in_kernel, part=part, layer=layer),
        grid=(m // tm, n // tn),
        in_specs=[pl.BlockSpec((tm, k), lambda i, j: (i, 0)),
                  pl.BlockSpec((k, tn), lambda i, j: (0, j)),
                  pl.BlockSpec((N_A_LAYERS, tn), lambda i, j: (0, j))],
        out_specs=out_specs,
        out_shape=out_shape,
        compiler_params=_params(("parallel", "parallel")),
        name="hgrn_in_" + part,
    )(a, w, lb_param)


def _scan_constants():
    c = SCAN_CHUNK
    t = np.arange(c)[None, :]
    i = np.arange(c)[:, None]
    blocks = [(t <= i), (t > i)]
    masks = [np.eye(c, dtype=np.float32)]
    j = np.arange(c)[None, :]
    for lvl in range(1, SCAN_LEVELS + 1):
        s = 1 << lvl
        h = s // 2
        m = (i // s) * s + h - 1
        query_role = (i % s) >= h
        blocks.append(np.where(query_role, (t > m) & (t <= i), (t > i) & (t <= m)))
        masks.append((((i // s) == (j // s)) & query_role & ((j % s) < h)).astype(np.float32))
    a = np.concatenate(blocks, axis=0).astype(np.float32)
    return np.concatenate([a, a], axis=1), np.stack(masks)


def _query_row_blocks(lvl):
    size = 1 << lvl
    nblk = SCAN_CHUNK // SCAN_ROWS
    if size <= SCAN_ROWS:
        return list(range(nblk))
    return [r for r in range(nblk) if (r * SCAN_ROWS) % size >= size // 2]


def _hgrn_scan_kernel(qs_ref, k_ref, l2f_ref, v_ref, gg_ref, gn_ref, sums_ref, mask_ref, o_ref, st_ref, ex_ref):
    c = SCAN_CHUNK
    rb = SCAN_ROWS
    nblk = c // rb
    gn = gn_ref[...]
    st_ref[...] = jnp.zeros_like(st_ref)

    def decay_stage(r0, buf):
        for pair in range(SCAN_HEADS // 2):
            cols = slice(pair * 2 * LANES, (pair + 1) * 2 * LANES)
            l2f = l2f_ref[pl.ds(r0, c), cols]
            hi = l2f.astype(BF16)
            lo = (l2f - hi.astype(F32)).astype(BF16)
            ex_ref[buf, :, cols] = jnp.exp2(_dot(sums_ref[...], jnp.concatenate([hi, lo], axis=0)))

    every = list(range(nblk))

    def decays(buf, hd, block, rows):
        cols = slice(hd * LANES, (hd + 1) * LANES)
        return jnp.concatenate(
            [ex_ref[buf, block * c + r * rb:block * c + (r + 1) * rb, cols] for r in rows], axis=0)

    def intra_scores(r0, hd, buf):
        cols = slice(hd * LANES, (hd + 1) * LANES)
        qs = qs_ref[pl.ds(r0, c), cols]
        k = k_ref[pl.ds(r0, c), cols]
        qs32 = qs.astype(F32)
        k32 = k.astype(F32)
        diag = _dot_nt(qs, k)
        sc = [diag[r * rb:(r + 1) * rb] * mask_ref[0, r * rb:(r + 1) * rb, :] for r in every]
        for lvl in range(1, SCAN_LEVELS + 1):
            rows = _query_row_blocks(lvl)
            ql = jnp.concatenate([qs32[r * rb:(r + 1) * rb] for r in rows], axis=0) * decays(buf, hd, lvl + 1, rows)
            part = _dot_nt(ql.astype(BF16), (k32 * decays(buf, hd, lvl + 1, every)).astype(BF16))
            for n, r in enumerate(rows):
                sc[r] = sc[r] + part[n * rb:(n + 1) * rb] * mask_ref[lvl, r * rb:(r + 1) * rb, :]
        return jnp.concatenate(sc, axis=0).astype(BF16)

    def state_step(r0, hd, buf, scores):
        cols = slice(hd * LANES, (hd + 1) * LANES)
        qs32 = qs_ref[pl.ds(r0, c), cols].astype(F32)
        k32 = k_ref[pl.ds(r0, c), cols].astype(F32)
        v = v_ref[pl.ds(r0, c), cols]
        eb = decays(buf, hd, 0, every)
        ek = decays(buf, hd, 1, every)
        st = st_ref[hd]
        o = _dot_nt((qs32 * eb).astype(BF16), st.astype(BF16)) + _dot(scores, v)
        st_ref[hd] = st * eb[c - 1:c, :] + _dot_tn(v, (k32 * ek).astype(BF16))
        on = o * lax.rsqrt(jnp.mean(o * o, axis=-1, keepdims=True) + RMS_EPS)
        o_ref[pl.ds(r0, c), cols] = (on * gn * gg_ref[pl.ds(r0, c), cols].astype(F32)).astype(o_ref.dtype)

    decay_stage(0, 0)

    def chunk_pair(cp, carry):
        for half in range(2):
            r0 = pl.multiple_of(cp * (2 * c) + half * c, c)
            scores = [intra_scores(r0, hd, half) for hd in range(SCAN_HEADS)]
            decay_stage(pl.multiple_of(jnp.minimum(r0 + c, SEQ - c), c), 1 - half)
            for hd in range(SCAN_HEADS):
                state_step(r0, hd, half, scores[hd])
        return carry

    lax.fori_loop(0, SEQ // (2 * c), chunk_pair, 0)


def _hgrn_scan(qs, k, l2f, v, gg, g_norm):
    sums_np, masks_np = _scan_constants()
    sums_m = jnp.asarray(sums_np, BF16)
    masks = jnp.asarray(masks_np, F32)
    width = SCAN_HEADS * LANES
    blk = pl.BlockSpec((SEQ, width), lambda b, hg: (b, hg))
    return pl.pallas_call(
        _hgrn_scan_kernel,
        grid=(BATCH, HG_HEADS // SCAN_HEADS),
        in_specs=[blk, blk, blk, blk, blk,
                  pl.BlockSpec((1, LANES), lambda b, hg: (0, 0)),
                  pl.BlockSpec(sums_m.shape, lambda b, hg: (0, 0)),
                  pl.BlockSpec(masks.shape, lambda b, hg: (0, 0, 0))],
        out_specs=blk,
        out_shape=jax.ShapeDtypeStruct((N_TOK, D_MODEL), BF16),
        scratch_shapes=[pltpu.VMEM((SCAN_HEADS, HG_DV, HG_DK), F32),
                        pltpu.VMEM((2, (SCAN_LEVELS + 2) * SCAN_CHUNK, width), F32)],
        compiler_params=_params(("parallel", "parallel")),
        name="hgrn_scan",
    )(qs, k, l2f, v, gg, g_norm.reshape(1, HG_DV), sums_m, masks)


def _layer_norm(y, g, b):
    mu = jnp.mean(y, axis=-1, keepdims=True)
    d = y - mu
    var = jnp.mean(d * d, axis=-1, keepdims=True)
    return d * lax.rsqrt(var + LN_EPS) * g + b


def _matmul_ln_kernel(a_ref, w_ref, bias_ref, res_ref, g_ref, b_ref, of_ref, ob_ref):
    mix = _dot(a_ref[...], w_ref[...]) + bias_ref[...]
    out = _layer_norm(DEEPNORM_ALPHA * res_ref[...] + mix, g_ref[...], b_ref[...])
    of_ref[...] = out
    ob_ref[...] = out.astype(BF16)


def _matmul_ln(a, w, bias, res, g, b, tm=256):
    m, k = a.shape
    n = w.shape[1]
    row = lambda width: pl.BlockSpec((1, width), lambda i: (0, 0))
    return pl.pallas_call(
        _matmul_ln_kernel,
        grid=(m // tm,),
        in_specs=[pl.BlockSpec((tm, k), lambda i: (i, 0)),
                  pl.BlockSpec((k, n), lambda i: (0, 0)),
                  row(n),
                  pl.BlockSpec((tm, n), lambda i: (i, 0)),
                  row(n), row(n)],
        out_specs=[pl.BlockSpec((tm, n), lambda i: (i, 0)), pl.BlockSpec((tm, n), lambda i: (i, 0))],
        out_shape=[jax.ShapeDtypeStruct((m, n), F32), jax.ShapeDtypeStruct((m, n), BF16)],
        compiler_params=_params(("parallel",)),
        name="out_proj_ln",
    )(a, w, bias.reshape(1, n), res, g.reshape(1, n), b.reshape(1, n))


def _proj_rotary_kernel(a_ref, w_ref, bias_ref, pos_ref, invf_ref, o_ref, *, rot_chunks, scale):
    acc = _dot(a_ref[...], w_ref[...]) + bias_ref[...]
    ang = pos_ref[...] * invf_ref[...]
    lane = lax.broadcasted_iota(jnp.int32, (1, LANES), 1) % ATT_HEAD_DIM
    half = ROT_DIM // 2
    cos = jnp.cos(ang)
    sin = jnp.sin(ang)
    c_keep = jnp.where(lane < ROT_DIM, cos, 1.0)
    s_from_left = jnp.where((lane >= half) & (lane < ROT_DIM), sin, 0.0)
    s_from_right = jnp.where(lane < half, -sin, 0.0)
    for j in range(acc.shape[1] // LANES):
        t = acc[:, j * LANES:(j + 1) * LANES]
        if j < rot_chunks:
            t = (t * c_keep + pltpu.roll(t, half, 1) * s_from_left
                 + pltpu.roll(t, LANES - half, 1) * s_from_right)
        o_ref[:, j * LANES:(j + 1) * LANES] = (t * scale).astype(o_ref.dtype)


def _proj_rotary(a, w, bias, pos, invf, rot_chunks, scale, tm=512, tn=512):
    m, k = a.shape
    n = w.shape[1]
    return pl.pallas_call(
        functools.partial(_proj_rotary_kernel, rot_chunks=rot_chunks, scale=scale),
        grid=(m // tm, n // tn),
        in_specs=[pl.BlockSpec((tm, k), lambda i, j: (i, 0)),
                  pl.BlockSpec((k, tn), lambda i, j: (0, j)),
                  pl.BlockSpec((1, tn), lambda i, j: (0, j)),
                  pl.BlockSpec((tm, 1), lambda i, j: (i, 0)),
                  pl.BlockSpec((1, LANES), lambda i, j: (0, 0))],
        out_specs=pl.BlockSpec((tm, tn), lambda i, j: (i, j)),
        out_shape=jax.ShapeDtypeStruct((m, n), BF16),
        compiler_params=_params(("parallel", "parallel")),
        name="proj_rotary",
    )(a, w, bias.reshape(1, n), pos, invf)


def _attn_kernel(sink_ref, q_ref, kp_ref, kc_ref, vp_ref, vc_ref, o_ref):
    nb = pl.program_id(1)
    w = WINDOW
    hd = ATT_HEAD_DIM
    k = jnp.concatenate([kp_ref[...], kc_ref[...]], axis=0)
    v = jnp.concatenate([vp_ref[...], vc_ref[...]], axis=0)
    qi = lax.broadcasted_iota(jnp.int32, (w, 2 * w), 0) + w
    ki = lax.broadcasted_iota(jnp.int32, (w, 2 * w), 1)
    rel = qi - ki
    valid = (rel >= 0) & (rel < w) & ((ki >= w) | (nb > 0))
    for j in range(ATT_KV_HEADS):
        kj = k[:, j * hd:(j + 1) * hd]
        vj = v[:, j * hd:(j + 1) * hd]
        heads = [j * ATT_GROUP + gq for gq in range(ATT_GROUP)]
        qw = q_ref[:, j * ATT_GROUP * hd:(j + 1) * ATT_GROUP * hd]
        qj = jnp.concatenate([qw[:, gq * hd:(gq + 1) * hd] for gq in range(ATT_GROUP)], axis=0)
        s_all = _dot_nt(qj, kj)
        outs = []
        for gq, h in enumerate(heads):
            s = jnp.where(valid, s_all[gq * w:(gq + 1) * w], MASK_VALUE)
            sink = sink_ref[h]
            m = jnp.maximum(jnp.max(s, axis=-1, keepdims=True), sink)
            p = jnp.exp(s - m)
            denom = jnp.sum(p, axis=-1, keepdims=True) + jnp.exp(sink - m)
            outs.append(_dot(p.astype(BF16), vj) / denom)
        for gq in range(0, ATT_GROUP, 2):
            h = heads[gq]
            o_ref[:, h * hd:(h + 2) * hd] = jnp.concatenate(outs[gq:gq + 2], axis=1).astype(o_ref.dtype)


def _attention(q, kv, sinks):
    nblk = SEQ // WINDOW
    kvw = ATT_KV_HEADS * ATT_HEAD_DIM
    cur = lambda col: pl.BlockSpec((WINDOW, kvw), lambda b, n, s: (b * nblk + n, col))
    prev = lambda col: pl.BlockSpec((WINDOW, kvw), lambda b, n, s: (b * nblk + jnp.maximum(n - 1, 0), col))
    grid_spec = pltpu.PrefetchScalarGridSpec(
        num_scalar_prefetch=1,
        grid=(BATCH, nblk),
        in_specs=[pl.BlockSpec((WINDOW, D_MODEL), lambda b, n, s: (b * nblk + n, 0)),
                  prev(0), cur(0), prev(1), cur(1)],
        out_specs=pl.BlockSpec((WINDOW, D_MODEL), lambda b, n, s: (b * nblk + n, 0)),
    )
    return pl.pallas_call(
        _attn_kernel,
        grid_spec=grid_spec,
        out_shape=jax.ShapeDtypeStruct((N_TOK, D_MODEL), BF16),
        compiler_params=_params(("parallel", "parallel")),
        name="swa_attention",
    )(sinks, q, kv, kv, kv, kv)


def _router_kernel(x_ref, wh_ref, wl_ref, bias_ref, idx_ref, gate_ref):
    x = x_ref[...]
    xh = x.astype(BF16)
    xl = (x - xh.astype(F32)).astype(BF16)
    wh = wh_ref[...]
    logits = _dot(xh, wh) + _dot(xl, wh) + _dot(xh, wl_ref[...])
    lt = logits.T[0:N_EXPERTS]
    scores = _sigmoid(lt)
    biased = scores + bias_ref[...]
    g = N_GROUPS
    a, b, c, d = (biased[p * g:(p + 1) * g] for p in range(EXPERTS_PER_GROUP))
    group_score = jnp.maximum(jnp.maximum(a + b, c + d), jnp.maximum(a, b) + jnp.maximum(c, d))
    gi = lax.broadcasted_iota(jnp.int32, group_score.shape, 0)
    best = jnp.min(jnp.where(group_score == jnp.max(group_score, axis=0, keepdims=True), gi, g),
                   axis=0, keepdims=True)
    row = lax.broadcasted_iota(jnp.int32, biased.shape, 0)
    expert = (row % g) * EXPERTS_PER_GROUP + row // g
    masked = jnp.where((row % g) == best, biased, ROUTE_MASK)

    def take_top(vals):
        top = jnp.max(vals, axis=0, keepdims=True)
        return jnp.min(jnp.where(vals == top, expert, N_EXPERTS), axis=0, keepdims=True)

    first = take_top(masked)
    second = take_top(jnp.where(expert == first, -jnp.inf, masked))
    w1 = jnp.sum(jnp.where(expert == first, scores, 0.0), axis=0, keepdims=True)
    w2 = jnp.sum(jnp.where(expert == second, scores, 0.0), axis=0, keepdims=True)
    idx_ref[0:1, :] = first
    idx_ref[1:2, :] = second
    gate_ref[0:1, :] = w1 / (w1 + w2)
    gate_ref[1:2, :] = w2 / (w1 + w2)


def _router(x, wh, wl, bias, tm=512):
    m, k = x.shape
    return pl.pallas_call(
        _router_kernel,
        grid=(m // tm,),
        in_specs=[pl.BlockSpec((tm, k), lambda i: (i, 0)),
                  pl.BlockSpec((k, LANES), lambda i: (0, 0)),
                  pl.BlockSpec((k, LANES), lambda i: (0, 0)),
                  pl.BlockSpec((N_EXPERTS, 1), lambda i: (0, 0))],
        out_specs=[pl.BlockSpec((TOP_K, tm), lambda i: (0, i)), pl.BlockSpec((TOP_K, tm), lambda i: (0, i))],
        out_shape=[jax.ShapeDtypeStruct((TOP_K, m), jnp.int32), jax.ShapeDtypeStruct((TOP_K, m), F32)],
        compiler_params=_params(("parallel",)),
        name="router",
    )(x, wh, wl, bias)


def _expert_kernel(be_ref, nused_ref, next_ref, src_ref, x_hbm, wgu_hbm, wd_hbm, y_ref,
                   wgu_f, wd_f, wgu_b, wd_b, xbuf, wsem, xsem, *, layer):
    i = pl.program_id(0)
    n_used = nused_ref[0]
    e = be_ref[i]
    used = i < n_used
    first = (i == 0) | (e != be_ref[jnp.maximum(i - 1, 0)])
    slot = i % 2

    def weight_copies(expert):
        return (pltpu.make_async_copy(wgu_hbm.at[layer, expert], wgu_f, wsem.at[0]),
                pltpu.make_async_copy(wd_hbm.at[layer, expert], wd_f, wsem.at[1]))

    def row_copy(tok, to_slot, r):
        return pltpu.make_async_copy(x_hbm.at[pl.ds(tok, 1)], xbuf.at[to_slot, pl.ds(r, 1)], xsem.at[to_slot])

    def start_rows(block, to_slot):
        base = block * MOE_ROWS

        def body(r, carry):
            row_copy(src_ref[base + r], to_slot, r).start()
            return carry

        lax.fori_loop(0, MOE_ROWS, body, 0, unroll=8)

    def wait_rows(of_slot):
        def body(r, carry):
            row_copy(0, of_slot, r).wait()
            return carry

        lax.fori_loop(0, MOE_ROWS, body, 0, unroll=8)

    @pl.when(i == 0)
    def _():
        for cp in weight_copies(e):
            cp.start(priority=WEIGHT_DMA_THREAD)
        start_rows(0, 0)

    @pl.when(used & first)
    def _():
        for cp in weight_copies(e):
            cp.wait()
        wgu_b[...] = wgu_f[...].astype(BF16)
        wd_b[...] = wd_f[...].astype(BF16)
        nxt = next_ref[i]

        @pl.when(nxt >= 0)
        def _():
            for cp in weight_copies(nxt):
                cp.start(priority=WEIGHT_DMA_THREAD)

    @pl.when(used)
    def _():
        @pl.when(i + 1 < n_used)
        def _():
            start_rows(i + 1, 1 - slot)

        wait_rows(slot)
        h = _dot(xbuf[slot].astype(BF16), wgu_b[...])
        hg = h[:, :D_FF]
        hu = h[:, D_FF:]
        act = (hg * _sigmoid(hg) * hu).astype(BF16)
        y_ref[...] = _dot(act, wd_b[...])

    @pl.when(jnp.logical_not(used))
    def _():
        y_ref[...] = jnp.zeros_like(y_ref)


def _expert_ffn(x, src_tok, block_expert, n_used, next_expert, w_gate_up, w_down, layer):
    any_space = pl.BlockSpec(memory_space=pl.ANY)
    grid_spec = pltpu.PrefetchScalarGridSpec(
        num_scalar_prefetch=4,
        grid=(MOE_BLOCKS,),
        in_specs=[any_space, any_space, any_space],
        out_specs=pl.BlockSpec((MOE_ROWS, D_MODEL), lambda i, *_: (i, 0)),
        scratch_shapes=[pltpu.VMEM((D_MODEL, 2 * D_FF), F32), pltpu.VMEM((D_FF, D_MODEL), F32),
                        pltpu.VMEM((D_MODEL, 2 * D_FF), BF16), pltpu.VMEM((D_FF, D_MODEL), BF16),
                        pltpu.VMEM((2, MOE_ROWS, D_MODEL), F32),
                        pltpu.SemaphoreType.DMA((2,)), pltpu.SemaphoreType.DMA((2,))],
    )
    return pl.pallas_call(
        functools.partial(_expert_kernel, layer=layer),
        grid_spec=grid_spec,
        out_shape=jax.ShapeDtypeStruct((MOE_PAD_ROWS, D_MODEL), F32),
        compiler_params=_params(("arbitrary",)),
        name="expert_ffn",
    )(block_expert, n_used, next_expert, src_tok, x, w_gate_up, w_down)


def _combine_ln_kernel(dest_ref, res_ref, gate_ref, g_ref, b_ref, y_hbm, of_ref, ob_ref, ybuf, ysem):
    i = pl.program_id(0)
    slot = i % 2
    tm = res_ref.shape[0]

    def row_copy(row, to_slot, k, r):
        return pltpu.make_async_copy(y_hbm.at[pl.ds(row, 1)], ybuf.at[to_slot, k, pl.ds(r, 1)], ysem.at[to_slot])

    def start_rows(base, to_slot, rows):
        for r in rows:
            for k in range(TOP_K):
                row_copy(dest_ref[base + r * TOP_K + k], to_slot, k, r).start()

    def wait_rows(of_slot):
        for r in range(tm):
            for k in range(TOP_K):
                row_copy(0, of_slot, k, r).wait()

    @pl.when(i == 0)
    def _():
        start_rows(0, 0, range(tm))

    wait_rows(slot)
    last = pl.num_programs(0) - 1
    base = jnp.minimum(i + 1, last) * (tm * TOP_K)
    g = g_ref[...]
    b = b_ref[...]
    for c in range(tm // COMBINE_ROWS):
        rows = slice(c * COMBINE_ROWS, (c + 1) * COMBINE_ROWS)
        gate = gate_ref[rows, :]
        ffn = ybuf[slot, 0, rows, :] * gate[:, 0:1] + ybuf[slot, 1, rows, :] * gate[:, 1:2]
        out = _layer_norm(DEEPNORM_ALPHA * res_ref[rows, :] + ffn, g, b)
        of_ref[rows, :] = out
        ob_ref[rows, :] = out.astype(BF16)
        start_rows(base, 1 - slot, range(c * COMBINE_ROWS, (c + 1) * COMBINE_ROWS))

    @pl.when(i == last)
    def _():
        wait_rows(1 - slot)


def _combine_ln(res, y_pad, dest, gate, g, b, tm=256):
    m, n = res.shape
    blk = pl.BlockSpec((tm, n), lambda i, d: (i, 0))
    row = pl.BlockSpec((1, n), lambda i, d: (0, 0))
    grid_spec = pltpu.PrefetchScalarGridSpec(
        num_scalar_prefetch=1,
        grid=(m // tm,),
        in_specs=[blk, pl.BlockSpec((tm, TOP_K), lambda i, d: (i, 0)), row, row,
                  pl.BlockSpec(memory_space=pl.ANY)],
        out_specs=[blk, blk],
        scratch_shapes=[pltpu.VMEM((2, TOP_K, tm, n), F32), pltpu.SemaphoreType.DMA((2,))],
    )
    return pl.pallas_call(
        _combine_ln_kernel,
        grid_spec=grid_spec,
        out_shape=[jax.ShapeDtypeStruct((m, n), F32), jax.ShapeDtypeStruct((m, n), BF16)],
        compiler_params=_params(("arbitrary",)),
        name="combine_ln",
    )(dest, res, gate, g.reshape(1, n), b.reshape(1, n), y_pad)


def _dispatch_tables(idx):
    e_flat = idx.T.reshape(-1)
    onehot = (e_flat[:, None] == jnp.arange(N_EXPERTS, dtype=jnp.int32)[None, :]).astype(jnp.int32)
    csum = jnp.cumsum(onehot, axis=0)
    rank = jnp.take_along_axis(csum, e_flat[:, None], axis=1)[:, 0] - 1
    counts = csum[-1]
    padded = (counts + MOE_ROWS - 1) // MOE_ROWS * MOE_ROWS
    pad_end = jnp.cumsum(padded)
    dest = (pad_end - padded)[e_flat] + rank
    tok = jnp.arange(N_TOK * TOP_K, dtype=jnp.int32) // TOP_K
    src_tok = jnp.zeros((MOE_PAD_ROWS,), jnp.int32).at[dest].set(tok)
    starts = jnp.arange(MOE_BLOCKS, dtype=jnp.int32) * MOE_ROWS
    block_expert = jnp.minimum(jnp.sum((pad_end[None, :] <= starts[:, None]).astype(jnp.int32), axis=1),
                               N_EXPERTS - 1)
    n_used = (pad_end[-1:] // MOE_ROWS).astype(jnp.int32)
    ids = jnp.arange(N_EXPERTS, dtype=jnp.int32)
    later = jnp.where((ids[None, :] > ids[:, None]) & (counts[None, :] > 0), ids[None, :], N_EXPERTS)
    next_nonempty = jnp.min(later, axis=1)
    next_expert = jnp.where(next_nonempty < N_EXPERTS, next_nonempty, -1)[block_expert].astype(jnp.int32)
    return dest, src_tok, block_expert, n_used, next_expert


def _moe_layer(xf, router_w, w_gate_up, w_down, ln_g, ln_b, layer):
    wh, wl, bias = router_w
    idx, gate = _router(xf, wh, wl, bias)
    dest, src_tok, block_expert, n_used, next_expert = _dispatch_tables(idx)
    y_pad = _expert_ffn(xf, src_tok, block_expert, n_used, next_expert, w_gate_up, w_down, layer)
    return _combine_ln(xf, y_pad, dest, gate.T, ln_g, ln_b)


def _router_weights(w_router, router_bias):
    lane = np.arange(N_EXPERTS)
    perm = (lane % N_GROUPS) * EXPERTS_PER_GROUP + lane // N_GROUPS
    w = jnp.pad(w_router[:, perm], ((0, 0), (0, LANES - N_EXPERTS)))
    wh = w.astype(BF16)
    wl = (w - wh.astype(F32)).astype(BF16)
    return wh, wl, router_bias[perm].astype(F32).reshape(N_EXPERTS, 1)


def kernel(x, positions, w_in_hgrn, lb_param, g_norm_hgrn, w_out_hgrn, w_kv, b_kv, w_q_attn, b_q_attn, sinks, w_o_attn, b_o_attn, w_router, router_bias, w_gate_up, w_down, ln_mix_g, ln_mix_b, ln_ffn_g, ln_ffn_b):
    xf = x.reshape(N_TOK, D_MODEL)
    xb = xf.astype(BF16)
    pos = positions.reshape(N_TOK, 1).astype(F32)
    half = ROT_DIM // 2
    inv_freq = 1.0 / (ROPE_THETA ** (jnp.arange(half, dtype=F32) * 2.0 / ROT_DIM))
    lane = np.arange(LANES) % ATT_HEAD_DIM
    invf = jnp.where(lane < ROT_DIM, inv_freq[lane % half], 0.0).reshape(1, LANES).astype(F32)
    router_w = _router_weights(w_router, router_bias)
    zero_bias = jnp.zeros((D_MODEL,), F32)
    kv = None
    for layer in range(DEPTH):
        if layer < N_A_LAYERS:
            w_in = w_in_hgrn[layer]
            lbp = lb_param.astype(F32)
            part = lambda name, n: _hgrn_in_proj(
                xb, w_in[:, n * D_MODEL:(n + 1) * D_MODEL].astype(BF16), lbp, name, layer)
            (qs,) = part("q", 0)
            l2f, kg = part("f", 1)
            (v,) = part("v", 2)
            (gg,) = part("g", 3)
            o = _hgrn_scan(qs, kg, l2f, v, gg, g_norm_hgrn[layer])
            xf, xb = _matmul_ln(o, w_out_hgrn[layer].astype(BF16), zero_bias, xf,
                                ln_mix_g[layer], ln_mix_b[layer])
        else:
            if layer == N_A_LAYERS:
                kv = _proj_rotary(xb, w_kv.astype(BF16), b_kv, pos, invf,
                                  rot_chunks=ATT_KV_HEADS * ATT_HEAD_DIM // LANES, scale=1.0)
            j = layer - N_A_LAYERS
            q = _proj_rotary(xb, w_q_attn[j].astype(BF16), b_q_attn[j], pos, invf,
                             rot_chunks=D_MODEL // LANES, scale=ATT_HEAD_DIM ** -0.5, tm=256, tn=D_MODEL)
            o = _attention(q, kv, sinks[j].astype(F32))
            xf, xb = _matmul_ln(o, w_o_attn[j].astype(BF16), b_o_attn[j], xf,
                                ln_mix_g[layer], ln_mix_b[layer])
        xf, xb = _moe_layer(xf, router_w, w_gate_up, w_down, ln_ffn_g[layer], ln_ffn_b[layer], layer)
    return xf.reshape(BATCH, SEQ, D_MODEL)
```

```python
import functools
import math

import numpy as np
import jax
import jax.numpy as jnp
from jax import lax
from jax.experimental import pallas as pl
from jax.experimental.pallas import tpu as pltpu

D_MODEL = 2048
BATCH = 4
SEQ = 2048
DEPTH = 4
N_TOK = BATCH * SEQ

N_A_LAYERS = DEPTH // 2
N_B_LAYERS = DEPTH - N_A_LAYERS
HG_EXPAND = 128
HG_HEADS = D_MODEL // HG_EXPAND
HG_DK = HG_EXPAND
HG_DV = D_MODEL // HG_HEADS
LB_FLOOR = 1e-30
ATT_HEAD_DIM = 64
ATT_Q_HEADS = D_MODEL // ATT_HEAD_DIM
ATT_KV_HEADS = ATT_Q_HEADS // 8
ATT_GROUP = ATT_Q_HEADS // ATT_KV_HEADS
WINDOW = 128
ROT_DIM = ATT_HEAD_DIM // 4
ROPE_THETA = 500000.0
MASK_VALUE = -1e30
N_EXPERTS = 32
N_GROUPS = 8
EXPERTS_PER_GROUP = N_EXPERTS // N_GROUPS
TOP_K = 2
D_FF = (704 * D_MODEL) // 2048
ROUTE_MASK = -1e9
DEEPNORM_ALPHA = (2 * DEPTH) ** 0.25
LN_EPS = 1e-5
RMS_EPS = 1e-6

LANES = 128
SCAN_CHUNK = 64
SCAN_LEVELS = 6
SCAN_HEADS = 4
SCAN_ROWS = 16
SCAN_GROUP = 4
MOE_ROWS = 256
MOE_BLOCKS = (N_TOK * TOP_K + N_EXPERTS * (MOE_ROWS - 1) + MOE_ROWS - 1) // MOE_ROWS
MOE_PAD_ROWS = MOE_BLOCKS * MOE_ROWS
WEIGHT_DMA_THREAD = 1
COMBINE_ROWS = 32
VMEM_LIMIT = 56 * 1024 * 1024

F32 = jnp.float32
BF16 = jnp.bfloat16


def _dot(a, b):
    return jnp.dot(a, b, preferred_element_type=F32)


def _dot_nt(a, b):
    return lax.dot_general(a, b, (((1,), (1,)), ((), ())), preferred_element_type=F32)


def _dot_tn(a, b):
    return lax.dot_general(a, b, (((0,), (0,)), ((), ())), preferred_element_type=F32)


def _sigmoid(x):
    return 1.0 / (1.0 + jnp.exp(-x))


def _params(sem, limit=VMEM_LIMIT):
    return pltpu.CompilerParams(dimension_semantics=sem, vmem_limit_bytes=limit)


def _hgrn_in_kernel(a_ref, w_ref, lbp_ref, *o_refs, part, layer):
    acc = _dot(a_ref[...], w_ref[...])
    if part == "q":
        o_refs[0][...] = (acc * _sigmoid(acc) * (HG_DK ** -0.5)).astype(BF16)
    elif part == "v":
        o_refs[0][...] = acc.astype(BF16)
    elif part == "g":
        o_refs[0][...] = (acc * _sigmoid(acc)).astype(BF16)
    else:
        lbp = lbp_ref[...]
        e = jnp.exp(lbp - jnp.max(lbp, axis=0, keepdims=True))
        sm = e / jnp.sum(e, axis=0, keepdims=True)
        lb = jnp.zeros_like(sm[0:1])
        for r in range(1, layer + 1):
            lb = lb + sm[r:r + 1]
        ez = jnp.exp(-jnp.abs(acc))
        rz = 1.0 / (1.0 + ez)
        sig_pos = jnp.where(acc >= 0, rz, ez * rz)
        sig_neg = jnp.where(acc >= 0, ez * rz, rz)
        forget = jnp.maximum(lb, LB_FLOOR) + (1.0 - lb) * sig_pos
        o_refs[0][...] = jnp.log(forget) * (1.0 / math.log(2.0))
        o_refs[1][...] = ((1.0 - lb) * sig_neg).astype(BF16)


def _hgrn_in_proj(a, w, lb_param, part, layer, tm=512, tn=1024):
    m, k = a.shape
    n = w.shape[1]
    blk = pl.BlockSpec((tm, tn), lambda i, j: (i, j))
    if part == "f":
        out_specs = [blk, blk]
        out_shape = [jax.ShapeDtypeStruct((m, n), F32), jax.ShapeDtypeStruct((m, n), BF16)]
    else:
        out_specs = [blk]
        out_shape = [jax.ShapeDtypeStruct((m, n), BF16)]
    return pl.pallas_call(
        functools.partial(_hgrn_in_kernel, part=part, layer=layer),
        grid=(m // tm, n // tn),
        in_specs=[pl.BlockSpec((tm, k), lambda i, j: (i, 0)),
                  pl.BlockSpec((k, tn), lambda i, j: (0, j)),
                  pl.BlockSpec((N_A_LAYERS, tn), lambda i, j: (0, j))],
        out_specs=out_specs,
        out_shape=out_shape,
        compiler_params=_params(("parallel", "parallel")),
        name="hgrn_in_" + part,
    )(a, w, lb_param)


def _scan_constants():
    c = SCAN_CHUNK
    t = np.arange(c)[None, :]
    i = np.arange(c)[:, None]
    blocks = [(t <= i), (t > i)]
    masks = [np.eye(c, dtype=np.float32)]
    j = np.arange(c)[None, :]
    for lvl in range(1, SCAN_LEVELS + 1):
        s = 1 << lvl
        h = s // 2
        m = (i // s) * s + h - 1
        query_role = (i % s) >= h
        blocks.append(np.where(query_role, (t > m) & (t <= i), (t > i) & (t <= m)))
        masks.append((((i // s) == (j // s)) & query_role & ((j % s) < h)).astype(np.float32))
    a = np.concatenate(blocks, axis=0).astype(np.float32)
    return np.concatenate([a, a], axis=1), np.stack(masks)


def _query_row_blocks(lvl):
    size = 1 << lvl
    nblk = SCAN_CHUNK // SCAN_ROWS
    if size <= SCAN_ROWS:
        return list(range(nblk))
    return [r for r in range(nblk) if (r * SCAN_ROWS) % size >= size // 2]


def _hgrn_scan_kernel(qs_ref, k_ref, l2f_ref, v_ref, gg_ref, gn_ref, sums_ref, mask_ref, o_ref, st_ref, ex_ref):
    c = SCAN_CHUNK
    rb = SCAN_ROWS
    nblk = c // rb
    gn = gn_ref[...]
    st_ref[...] = jnp.zeros_like(st_ref)

    def decay_stage(r0, buf):
        for pair in range(SCAN_HEADS // 2):
            cols = slice(pair * 2 * LANES, (pair + 1) * 2 * LANES)
            l2f = l2f_ref[pl.ds(r0, c), cols]
            hi = l2f.astype(BF16)
            lo = (l2f - hi.astype(F32)).astype(BF16)
            ex_ref[buf, :, cols] = jnp.exp2(_dot(sums_ref[...], jnp.concatenate([hi, lo], axis=0)))

    every = list(range(nblk))

    def decays(buf, hd, block, rows):
        cols = slice(hd * LANES, (hd + 1) * LANES)
        return jnp.concatenate(
            [ex_ref[buf, block * c + r * rb:block * c + (r + 1) * rb, cols] for r in rows], axis=0)

    def intra_scores(r0, hd, buf):
        cols = slice(hd * LANES, (hd + 1) * LANES)
        qs = qs_ref[pl.ds(r0, c), cols]
        k = k_ref[pl.ds(r0, c), cols]
        qs32 = qs.astype(F32)
        k32 = k.astype(F32)
        diag = _dot_nt(qs, k)
        sc = [diag[r * rb:(r + 1) * rb] * mask_ref[0, r * rb:(r + 1) * rb, :] for r in every]
        for lvl in range(1, SCAN_LEVELS + 1):
            rows = _query_row_blocks(lvl)
            ql = jnp.concatenate([qs32[r * rb:(r + 1) * rb] for r in rows], axis=0) * decays(buf, hd, lvl + 1, rows)
            part = _dot_nt(ql.astype(BF16), (k32 * decays(buf, hd, lvl + 1, every)).astype(BF16))
            for n, r in enumerate(rows):
                sc[r] = sc[r] + part[n * rb:(n + 1) * rb] * mask_ref[lvl, r * rb:(r + 1) * rb, :]
        return jnp.concatenate(sc, axis=0).astype(BF16)

    def state_step(r0, hd, buf, scores):
        cols = slice(hd * LANES, (hd + 1) * LANES)
        qs32 = qs_ref[pl.ds(r0, c), cols].astype(F32)
        k32 = k_ref[pl.ds(r0, c), cols].astype(F32)
        v = v_ref[pl.ds(r0, c), cols]
        eb = decays(buf, hd, 0, every)
        ek = decays(buf, hd, 1, every)
        st = st_ref[hd]
        o = _dot_nt((qs32 * eb).astype(BF16), st.astype(BF16)) + _dot(scores, v)
        st_ref[hd] = st * eb[c - 1:c, :] + _dot_tn(v, (k32 * ek).astype(BF16))
        on = o * lax.rsqrt(jnp.mean(o * o, axis=-1, keepdims=True) + RMS_EPS)
        o_ref[pl.ds(r0, c), cols] = (on * gn * gg_ref[pl.ds(r0, c), cols].astype(F32)).astype(o_ref.dtype)

    decay_stage(0, 0)

    def chunk_group(cg, carry):
        for n in range(SCAN_GROUP):
            half = n % 2
            r0 = pl.multiple_of(cg * (SCAN_GROUP * c) + n * c, c)
            scores = [intra_scores(r0, hd, half) for hd in range(SCAN_HEADS)]
            decay_stage(pl.multiple_of(jnp.minimum(r0 + c, SEQ - c), c), 1 - half)
            for hd in range(SCAN_HEADS):
                state_step(r0, hd, half, scores[hd])
        return carry

    lax.fori_loop(0, SEQ // (SCAN_GROUP * c), chunk_group, 0)


def _hgrn_scan(qs, k, l2f, v, gg, g_norm):
    sums_np, masks_np = _scan_constants()
    sums_m = jnp.asarray(sums_np, BF16)
    masks = jnp.asarray(masks_np, F32)
    width = SCAN_HEADS * LANES
    blk = pl.BlockSpec((SEQ, width), lambda b, hg: (b, hg))
    return pl.pallas_call(
        _hgrn_scan_kernel,
        grid=(BATCH, HG_HEADS // SCAN_HEADS),
        in_specs=[blk, blk, blk, blk, blk,
                  pl.BlockSpec((1, LANES), lambda b, hg: (0, 0)),
                  pl.BlockSpec(sums_m.shape, lambda b, hg: (0, 0)),
                  pl.BlockSpec(masks.shape, lambda b, hg: (0, 0, 0))],
        out_specs=blk,
        out_shape=jax.ShapeDtypeStruct((N_TOK, D_MODEL), BF16),
        scratch_shapes=[pltpu.VMEM((SCAN_HEADS, HG_DV, HG_DK), F32),
                        pltpu.VMEM((2, (SCAN_LEVELS + 2) * SCAN_CHUNK, width), F32)],
        compiler_params=_params(("parallel", "parallel")),
        name="hgrn_scan",
    )(qs, k, l2f, v, gg, g_norm.reshape(1, HG_DV), sums_m, masks)


def _layer_norm(y, g, b):
    mu = jnp.mean(y, axis=-1, keepdims=True)
    d = y - mu
    var = jnp.mean(d * d, axis=-1, keepdims=True)
    return d * lax.rsqrt(var + LN_EPS) * g + b


def _matmul_ln_kernel(a_ref, w_ref, bias_ref, res_ref, g_ref, b_ref, of_ref, ob_ref):
    mix = _dot(a_ref[...], w_ref[...]) + bias_ref[...]
    out = _layer_norm(DEEPNORM_ALPHA * res_ref[...] + mix, g_ref[...], b_ref[...])
    of_ref[...] = out
    ob_ref[...] = out.astype(BF16)


def _matmul_ln(a, w, bias, res, g, b, tm=256):
    m, k = a.shape
    n = w.shape[1]
    row = lambda width: pl.BlockSpec((1, width), lambda i: (0, 0))
    return pl.pallas_call(
        _matmul_ln_kernel,
        grid=(m // tm,),
        in_specs=[pl.BlockSpec((tm, k), lambda i: (i, 0)),
                  pl.BlockSpec((k, n), lambda i: (0, 0)),
                  row(n),
                  pl.BlockSpec((tm, n), lambda i: (i, 0)),
                  row(n), row(n)],
        out_specs=[pl.BlockSpec((tm, n), lambda i: (i, 0)), pl.BlockSpec((tm, n), lambda i: (i, 0))],
        out_shape=[jax.ShapeDtypeStruct((m, n), F32), jax.ShapeDtypeStruct((m, n), BF16)],
        compiler_params=_params(("parallel",)),
        name="out_proj_ln",
    )(a, w, bias.reshape(1, n), res, g.reshape(1, n), b.reshape(1, n))


def _proj_rotary_kernel(a_ref, w_ref, bias_ref, pos_ref, invf_ref, o_ref, *, rot_chunks, scale):
    acc = _dot(a_ref[...], w_ref[...]) + bias_ref[...]
    ang = pos_ref[...] * invf_ref[...]
    lane = lax.broadcasted_iota(jnp.int32, (1, LANES), 1) % ATT_HEAD_DIM
    half = ROT_DIM // 2
    cos = jnp.cos(ang)
    sin = jnp.sin(ang)
    c_keep = jnp.where(lane < ROT_DIM, cos, 1.0)
    s_from_left = jnp.where((lane >= half) & (lane < ROT_DIM), sin, 0.0)
    s_from_right = jnp.where(lane < half, -sin, 0.0)
    for j in range(acc.shape[1] // LANES):
        t = acc[:, j * LANES:(j + 1) * LANES]
        if j < rot_chunks:
            t = (t * c_keep + pltpu.roll(t, half, 1) * s_from_left
                 + pltpu.roll(t, LANES - half, 1) * s_from_right)
        o_ref[:, j * LANES:(j + 1) * LANES] = (t * scale).astype(o_ref.dtype)


def _proj_rotary(a, w, bias, pos, invf, rot_chunks, scale, tm=512, tn=512):
    m, k = a.shape
    n = w.shape[1]
    return pl.pallas_call(
        functools.partial(_proj_rotary_kernel, rot_chunks=rot_chunks, scale=scale),
        grid=(m // tm, n // tn),
        in_specs=[pl.BlockSpec((tm, k), lambda i, j: (i, 0)),
                  pl.BlockSpec((k, tn), lambda i, j: (0, j)),
                  pl.BlockSpec((1, tn), lambda i, j: (0, j)),
                  pl.BlockSpec((tm, 1), lambda i, j: (i, 0)),
                  pl.BlockSpec((1, LANES), lambda i, j: (0, 0))],
        out_specs=pl.BlockSpec((tm, tn), lambda i, j: (i, j)),
        out_shape=jax.ShapeDtypeStruct((m, n), BF16),
        compiler_params=_params(("parallel", "parallel")),
        name="proj_rotary",
    )(a, w, bias.reshape(1, n), pos, invf)


def _attn_kernel(sink_ref, q_ref, kp_ref, kc_ref, vp_ref, vc_ref, o_ref):
    nb = pl.program_id(1)
    w = WINDOW
    hd = ATT_HEAD_DIM
    k = jnp.concatenate([kp_ref[...], kc_ref[...]], axis=0)
    v = jnp.concatenate([vp_ref[...], vc_ref[...]], axis=0)
    qi = lax.broadcasted_iota(jnp.int32, (w, 2 * w), 0) + w
    ki = lax.broadcasted_iota(jnp.int32, (w, 2 * w), 1)
    rel = qi - ki
    valid = (rel >= 0) & (rel < w) & ((ki >= w) | (nb > 0))
    for j in range(ATT_KV_HEADS):
        kj = k[:, j * hd:(j + 1) * hd]
        vj = v[:, j * hd:(j + 1) * hd]
        heads = [j * ATT_GROUP + gq for gq in range(ATT_GROUP)]
        qw = q_ref[:, j * ATT_GROUP * hd:(j + 1) * ATT_GROUP * hd]
        qj = jnp.concatenate([qw[:, gq * hd:(gq + 1) * hd] for gq in range(ATT_GROUP)], axis=0)
        s_all = _dot_nt(qj, kj)
        outs = []
        for gq, h in enumerate(heads):
            s = jnp.where(valid, s_all[gq * w:(gq + 1) * w], MASK_VALUE)
            sink = sink_ref[h]
            m = jnp.maximum(jnp.max(s, axis=-1, keepdims=True), sink)
            p = jnp.exp(s - m)
            denom = jnp.sum(p, axis=-1, keepdims=True) + jnp.exp(sink - m)
            outs.append(_dot(p.astype(BF16), vj) / denom)
        for gq in range(0, ATT_GROUP, 2):
            h = heads[gq]
            o_ref[:, h * hd:(h + 2) * hd] = jnp.concatenate(outs[gq:gq + 2], axis=1).astype(o_ref.dtype)


def _attention(q, kv, sinks):
    nblk = SEQ // WINDOW
    kvw = ATT_KV_HEADS * ATT_HEAD_DIM
    cur = lambda col: pl.BlockSpec((WINDOW, kvw), lambda b, n, s: (b * nblk + n, col))
    prev = lambda col: pl.BlockSpec((WINDOW, kvw), lambda b, n, s: (b * nblk + jnp.maximum(n - 1, 0), col))
    grid_spec = pltpu.PrefetchScalarGridSpec(
        num_scalar_prefetch=1,
        grid=(BATCH, nblk),
        in_specs=[pl.BlockSpec((WINDOW, D_MODEL), lambda b, n, s: (b * nblk + n, 0)),
                  prev(0), cur(0), prev(1), cur(1)],
        out_specs=pl.BlockSpec((WINDOW, D_MODEL), lambda b, n, s: (b * nblk + n, 0)),
    )
    return pl.pallas_call(
        _attn_kernel,
        grid_spec=grid_spec,
        out_shape=jax.ShapeDtypeStruct((N_TOK, D_MODEL), BF16),
        compiler_params=_params(("parallel", "parallel")),
        name="swa_attention",
    )(sinks, q, kv, kv, kv, kv)


def _router_kernel(x_ref, wh_ref, wl_ref, bias_ref, idx_ref, gate_ref):
    x = x_ref[...]
    xh = x.astype(BF16)
    xl = (x - xh.astype(F32)).astype(BF16)
    wh = wh_ref[...]
    logits = _dot(xh, wh) + _dot(xl, wh) + _dot(xh, wl_ref[...])
    lt = logits.T[0:N_EXPERTS]
    scores = _sigmoid(lt)
    biased = scores + bias_ref[...]
    g = N_GROUPS
    a, b, c, d = (biased[p * g:(p + 1) * g] for p in range(EXPERTS_PER_GROUP))
    group_score = jnp.maximum(jnp.maximum(a + b, c + d), jnp.maximum(a, b) + jnp.maximum(c, d))
    gi = lax.broadcasted_iota(jnp.int32, group_score.shape, 0)
    best = jnp.min(jnp.where(group_score == jnp.max(group_score, axis=0, keepdims=True), gi, g),
                   axis=0, keepdims=True)
    row = lax.broadcasted_iota(jnp.int32, biased.shape, 0)
    expert = (row % g) * EXPERTS_PER_GROUP + row // g
    masked = jnp.where((row % g) == best, biased, ROUTE_MASK)

    def take_top(vals):
        top = jnp.max(vals, axis=0, keepdims=True)
        return jnp.min(jnp.where(vals == top, expert, N_EXPERTS), axis=0, keepdims=True)

    first = take_top(masked)
    second = take_top(jnp.where(expert == first, -jnp.inf, masked))
    w1 = jnp.sum(jnp.where(expert == first, scores, 0.0), axis=0, keepdims=True)
    w2 = jnp.sum(jnp.where(expert == second, scores, 0.0), axis=0, keepdims=True)
    idx_ref[0:1, :] = first
    idx_ref[1:2, :] = second
    gate_ref[0:1, :] = w1 / (w1 + w2)
    gate_ref[1:2, :] = w2 / (w1 + w2)


def _router(x, wh, wl, bias, tm=512):
    m, k = x.shape
    return pl.pallas_call(
        _router_kernel,
        grid=(m // tm,),
        in_specs=[pl.BlockSpec((tm, k), lambda i: (i, 0)),
                  pl.BlockSpec((k, LANES), lambda i: (0, 0)),
                  pl.BlockSpec((k, LANES), lambda i: (0, 0)),
                  pl.BlockSpec((N_EXPERTS, 1), lambda i: (0, 0))],
        out_specs=[pl.BlockSpec((TOP_K, tm), lambda i: (0, i)), pl.BlockSpec((TOP_K, tm), lambda i: (0, i))],
        out_shape=[jax.ShapeDtypeStruct((TOP_K, m), jnp.int32), jax.ShapeDtypeStruct((TOP_K, m), F32)],
        compiler_params=_params(("parallel",)),
        name="router",
    )(x, wh, wl, bias)


def _expert_kernel(be_ref, nused_ref, next_ref, src_ref, x_hbm, wgu_hbm, wd_hbm, y_ref,
                   wgu_f, wd_f, wgu_b, wd_b, xbuf, wsem, xsem, *, layer):
    i = pl.program_id(0)
    n_used = nused_ref[0]
    e = be_ref[i]
    used = i < n_used
    first = (i == 0) | (e != be_ref[jnp.maximum(i - 1, 0)])
    slot = i % 2

    def weight_copies(expert):
        return (pltpu.make_async_copy(wgu_hbm.at[layer, expert], wgu_f, wsem.at[0]),
                pltpu.make_async_copy(wd_hbm.at[layer, expert], wd_f, wsem.at[1]))

    def row_copy(tok, to_slot, r):
        return pltpu.make_async_copy(x_hbm.at[pl.ds(tok, 1)], xbuf.at[to_slot, pl.ds(r, 1)], xsem.at[to_slot])

    def start_rows(block, to_slot):
        base = block * MOE_ROWS

        def body(r, carry):
            row_copy(src_ref[base + r], to_slot, r).start()
            return carry

        lax.fori_loop(0, MOE_ROWS, body, 0, unroll=8)

    def wait_rows(of_slot):
        def body(r, carry):
            row_copy(0, of_slot, r).wait()
            return carry

        lax.fori_loop(0, MOE_ROWS, body, 0, unroll=8)

    @pl.when(i == 0)
    def _():
        for cp in weight_copies(e):
            cp.start(priority=WEIGHT_DMA_THREAD)
        start_rows(0, 0)

    @pl.when(used & first)
    def _():
        for cp in weight_copies(e):
            cp.wait()
        wgu_b[...] = wgu_f[...].astype(BF16)
        wd_b[...] = wd_f[...].astype(BF16)
        nxt = next_ref[i]

        @pl.when(nxt >= 0)
        def _():
            for cp in weight_copies(nxt):
                cp.start(priority=WEIGHT_DMA_THREAD)

    @pl.when(used)
    def _():
        @pl.when(i + 1 < n_used)
        def _():
            start_rows(i + 1, 1 - slot)

        wait_rows(slot)
        h = _dot(xbuf[slot].astype(BF16), wgu_b[...])
        hg = h[:, :D_FF]
        hu = h[:, D_FF:]
        act = (hg * _sigmoid(hg) * hu).astype(BF16)
        y_ref[...] = _dot(act, wd_b[...])

    @pl.when(jnp.logical_not(used))
    def _():
        y_ref[...] = jnp.zeros_like(y_ref)


def _expert_ffn(x, src_tok, block_expert, n_used, next_expert, w_gate_up, w_down, layer):
    any_space = pl.BlockSpec(memory_space=pl.ANY)
    grid_spec = pltpu.PrefetchScalarGridSpec(
        num_scalar_prefetch=4,
        grid=(MOE_BLOCKS,),
        in_specs=[any_space, any_space, any_space],
        out_specs=pl.BlockSpec((MOE_ROWS, D_MODEL), lambda i, *_: (i, 0)),
        scratch_shapes=[pltpu.VMEM((D_MODEL, 2 * D_FF), F32), pltpu.VMEM((D_FF, D_MODEL), F32),
                        pltpu.VMEM((D_MODEL, 2 * D_FF), BF16), pltpu.VMEM((D_FF, D_MODEL), BF16),
                        pltpu.VMEM((2, MOE_ROWS, D_MODEL), F32),
                        pltpu.SemaphoreType.DMA((2,)), pltpu.SemaphoreType.DMA((2,))],
    )
    return pl.pallas_call(
        functools.partial(_expert_kernel, layer=layer),
        grid_spec=grid_spec,
        out_shape=jax.ShapeDtypeStruct((MOE_PAD_ROWS, D_MODEL), F32),
        compiler_params=_params(("arbitrary",)),
        name="expert_ffn",
    )(block_expert, n_used, next_expert, src_tok, x, w_gate_up, w_down)


def _combine_ln_kernel(dest_ref, res_ref, gate_ref, g_ref, b_ref, y_hbm, of_ref, ob_ref, ybuf, ysem):
    i = pl.program_id(0)
    slot = i % 2
    tm = res_ref.shape[0]

    def row_copy(row, to_slot, k, r):
        return pltpu.make_async_copy(y_hbm.at[pl.ds(row, 1)], ybuf.at[to_slot, k, pl.ds(r, 1)], ysem.at[to_slot])

    def start_rows(base, to_slot, rows):
        for r in rows:
            for k in range(TOP_K):
                row_copy(dest_ref[base + r * TOP_K + k], to_slot, k, r).start()

    def wait_rows(of_slot):
        for r in range(tm):
            for k in range(TOP_K):
                row_copy(0, of_slot, k, r).wait()

    @pl.when(i == 0)
    def _():
        start_rows(0, 0, range(tm))

    wait_rows(slot)
    last = pl.num_programs(0) - 1
    base = jnp.minimum(i + 1, last) * (tm * TOP_K)
    g = g_ref[...]
    b = b_ref[...]
    for c in range(tm // COMBINE_ROWS):
        rows = slice(c * COMBINE_ROWS, (c + 1) * COMBINE_ROWS)
        gate = gate_ref[rows, :]
        ffn = ybuf[slot, 0, rows, :] * gate[:, 0:1] + ybuf[slot, 1, rows, :] * gate[:, 1:2]
        out = _layer_norm(DEEPNORM_ALPHA * res_ref[rows, :] + ffn, g, b)
        of_ref[rows, :] = out
        ob_ref[rows, :] = out.astype(BF16)
        start_rows(base, 1 - slot, range(c * COMBINE_ROWS, (c + 1) * COMBINE_ROWS))

    @pl.when(i == last)
    def _():
        wait_rows(1 - slot)


def _combine_ln(res, y_pad, dest, gate, g, b, tm=256):
    m, n = res.shape
    blk = pl.BlockSpec((tm, n), lambda i, d: (i, 0))
    row = pl.BlockSpec((1, n), lambda i, d: (0, 0))
    grid_spec = pltpu.PrefetchScalarGridSpec(
        num_scalar_prefetch=1,
        grid=(m // tm,),
        in_specs=[blk, pl.BlockSpec((tm, TOP_K), lambda i, d: (i, 0)), row, row,
                  pl.BlockSpec(memory_space=pl.ANY)],
        out_specs=[blk, blk],
        scratch_shapes=[pltpu.VMEM((2, TOP_K, tm, n), F32), pltpu.SemaphoreType.DMA((2,))],
    )
    return pl.pallas_call(
        _combine_ln_kernel,
        grid_spec=grid_spec,
        out_shape=[jax.ShapeDtypeStruct((m, n), F32), jax.ShapeDtypeStruct((m, n), BF16)],
        compiler_params=_params(("arbitrary",)),
        name="combine_ln",
    )(dest, res, gate, g.reshape(1, n), b.reshape(1, n), y_pad)


def _source_rows_kernel(dest_ref, src_ref):
    def clear(r, carry):
        src_ref[r] = 0
        return carry

    def place(p, carry):
        src_ref[dest_ref[p]] = lax.shift_right_logical(p, TOP_K.bit_length() - 1)
        return carry

    assert TOP_K & (TOP_K - 1) == 0

    lax.fori_loop(0, MOE_PAD_ROWS, clear, 0, unroll=8)
    lax.fori_loop(0, N_TOK * TOP_K, place, 0, unroll=8)


def _source_rows(dest):
    return pl.pallas_call(
        _source_rows_kernel,
        in_specs=[pl.BlockSpec(memory_space=pltpu.SMEM)],
        out_specs=pl.BlockSpec(memory_space=pltpu.SMEM),
        out_shape=jax.ShapeDtypeStruct((MOE_PAD_ROWS,), jnp.int32),
        name="source_rows",
    )(dest)


def _dispatch_tables(idx):
    e_flat = idx.T.reshape(-1)
    onehot = (e_flat[:, None] == jnp.arange(N_EXPERTS, dtype=jnp.int32)[None, :]).astype(jnp.int32)
    csum = jnp.cumsum(onehot, axis=0)
    rank = jnp.take_along_axis(csum, e_flat[:, None], axis=1)[:, 0] - 1
    counts = csum[-1]
    padded = (counts + MOE_ROWS - 1) // MOE_ROWS * MOE_ROWS
    pad_end = jnp.cumsum(padded)
    dest = (pad_end - padded)[e_flat] + rank
    src_tok = _source_rows(dest)
    starts = jnp.arange(MOE_BLOCKS, dtype=jnp.int32) * MOE_ROWS
    block_expert = jnp.minimum(jnp.sum((pad_end[None, :] <= starts[:, None]).astype(jnp.int32), axis=1),
                               N_EXPERTS - 1)
    n_used = (pad_end[-1:] // MOE_ROWS).astype(jnp.int32)
    ids = jnp.arange(N_EXPERTS, dtype=jnp.int32)
    later = jnp.where((ids[None, :] > ids[:, None]) & (counts[None, :] > 0), ids[None, :], N_EXPERTS)
    next_nonempty = jnp.min(later, axis=1)
    next_expert = jnp.where(next_nonempty < N_EXPERTS, next_nonempty, -1)[block_expert].astype(jnp.int32)
    return dest, src_tok, block_expert, n_used, next_expert


def _moe_layer(xf, router_w, w_gate_up, w_down, ln_g, ln_b, layer):
    wh, wl, bias = router_w
    idx, gate = _router(xf, wh, wl, bias)
    dest, src_tok, block_expert, n_used, next_expert = _dispatch_tables(idx)
    y_pad = _expert_ffn(xf, src_tok, block_expert, n_used, next_expert, w_gate_up, w_down, layer)
    return _combine_ln(xf, y_pad, dest, gate.T, ln_g, ln_b)


def _router_weights(w_router, router_bias):
    lane = np.arange(N_EXPERTS)
    perm = (lane % N_GROUPS) * EXPERTS_PER_GROUP + lane // N_GROUPS
    w = jnp.pad(w_router[:, perm], ((0, 0), (0, LANES - N_EXPERTS)))
    wh = w.astype(BF16)
    wl = (w - wh.astype(F32)).astype(BF16)
    return wh, wl, router_bias[perm].astype(F32).reshape(N_EXPERTS, 1)


def kernel(x, positions, w_in_hgrn, lb_param, g_norm_hgrn, w_out_hgrn, w_kv, b_kv, w_q_attn, b_q_attn, sinks, w_o_attn, b_o_attn, w_router, router_bias, w_gate_up, w_down, ln_mix_g, ln_mix_b, ln_ffn_g, ln_ffn_b):
    xf = x.reshape(N_TOK, D_MODEL)
    xb = xf.astype(BF16)
    pos = positions.reshape(N_TOK, 1).astype(F32)
    half = ROT_DIM // 2
    inv_freq = 1.0 / (ROPE_THETA ** (jnp.arange(half, dtype=F32) * 2.0 / ROT_DIM))
    lane = np.arange(LANES) % ATT_HEAD_DIM
    invf = jnp.where(lane < ROT_DIM, inv_freq[lane % half], 0.0).reshape(1, LANES).astype(F32)
    router_w = _router_weights(w_router, router_bias)
    zero_bias = jnp.zeros((D_MODEL,), F32)
    kv = None
    for layer in range(DEPTH):
        if layer < N_A_LAYERS:
            w_in = w_in_hgrn[layer]
            lbp = lb_param.astype(F32)
            part = lambda name, n: _hgrn_in_proj(
                xb, w_in[:, n * D_MODEL:(n + 1) * D_MODEL].astype(BF16), lbp, name, layer)
            (qs,) = part("q", 0)
            l2f, kg = part("f", 1)
            (v,) = part("v", 2)
            (gg,) = part("g", 3)
            o = _hgrn_scan(qs, kg, l2f, v, gg, g_norm_hgrn[layer])
            xf, xb = _matmul_ln(o, w_out_hgrn[layer].astype(BF16), zero_bias, xf,
                                ln_mix_g[layer], ln_mix_b[layer])
        else:
            if layer == N_A_LAYERS:
                kv = _proj_rotary(xb, w_kv.astype(BF16), b_kv, pos, invf,
                                  rot_chunks=ATT_KV_HEADS * ATT_HEAD_DIM // LANES, scale=1.0)
            j = layer - N_A_LAYERS
            q = _proj_rotary(xb, w_q_attn[j].astype(BF16), b_q_attn[j], pos, invf,
                             rot_chunks=D_MODEL // LANES, scale=ATT_HEAD_DIM ** -0.5, tm=256, tn=D_MODEL)
            o = _attention(q, kv, sinks[j].astype(F32))
            xf, xb = _matmul_ln(o, w_o_attn[j].astype(BF16), b_o_attn[j], xf,
                                ln_mix_g[layer], ln_mix_b[layer])
        xf, xb = _moe_layer(xf, router_w, w_gate_up, w_down, ln_ffn_g[layer], ln_ffn_b[layer], layer)
    return xf.reshape(BATCH, SEQ, D_MODEL)
```
